```python
import math
import jax
import jax.numpy as jnp
from jax import lax
import numpy as np

D_MODEL = 2048
BATCH = 32
SEQ = 256
DEPTH = 2
DEC_BATCH = 2
DEC_SEQ = 1024
PAST_LEN = 512

GRID_W = 64
DK_GDN = 128
DV_GDN = 128
H_GDN = D_MODEL // (2 * DV_GDN)
GDN_W = H_GDN * DV_GDN
HEAD_DIM = 128
N_Q = D_MODEL // (2 * HEAD_DIM)
N_KV = N_Q // 4
ATTN_W = N_Q * HEAD_DIM
MIX_W = GDN_W + ATTN_W
CONV_K = 5
CHUNK = 64
Q_BLOCK = 128
D_FF = ((8 * D_MODEL + 767) // 768) * 256
ROPE_BASE = 10000.0
RMS_EPS = 1e-6
LN_EPS = 1e-5
DN_ALPHA = (2 * DEPTH) ** 0.25
DN_BETA = (8 * DEPTH) ** -0.25
QKV_W = 2 * H_GDN * DK_GDN + GDN_W
IN_W = QKV_W + GDN_W + 4 * H_GDN + ATTN_W + 2 * N_KV * HEAD_DIM

kernel_name = 'hybrid_gdn_gqa_diffusion_step'


def rms_norm(x, w):
    xf = x.astype(jnp.float32)
    y = xf * lax.rsqrt(jnp.mean(xf * xf, axis=-1, keepdims=True) + RMS_EPS)
    return (y * w.astype(jnp.float32)).astype(x.dtype)


def layer_norm(x, g, b):
    xf = x.astype(jnp.float32)
    mu = jnp.mean(xf, axis=-1, keepdims=True)
    var = jnp.mean(jnp.square(xf - mu), axis=-1, keepdims=True)
    y = (xf - mu) * lax.rsqrt(var + LN_EPS)
    return (y * g.astype(jnp.float32) + b.astype(jnp.float32)).astype(x.dtype)


def l2_normalize(x):
    return x * lax.rsqrt(jnp.sum(x * x, axis=-1, keepdims=True) + RMS_EPS)


def axial_rope_tables(n_tok):
    rows = n_tok // GRID_W
    row = jnp.repeat(jnp.arange(rows), GRID_W).astype(jnp.float32)
    col = jnp.tile(jnp.arange(GRID_W), rows).astype(jnp.float32)
    half = HEAD_DIM // 2
    inv_freq = ROPE_BASE ** (-jnp.arange(0, half, 2, dtype=jnp.float32) / half)
    ang_r = row[:, None] * inv_freq
    ang_c = col[:, None] * inv_freq
    return (jnp.cos(ang_r), jnp.sin(ang_r), jnp.cos(ang_c), jnp.sin(ang_c))


def _rotate(x, cos, sin):
    n = x.shape[-1] // 2
    x1 = x[..., :n].astype(jnp.float32)
    x2 = x[..., n:].astype(jnp.float32)
    c = cos[:, None, :]
    s = sin[:, None, :]
    return jnp.concatenate([x1 * c - x2 * s, x2 * c + x1 * s], axis=-1)


def apply_axial_rope(x, tabs):
    cos_r, sin_r, cos_c, sin_c = tabs
    half = HEAD_DIM // 2
    y = jnp.concatenate([_rotate(x[..., :half], cos_r, sin_r),
                         _rotate(x[..., half:], cos_c, sin_c)], axis=-1)
    return y.astype(x.dtype)


def short_conv(x, w):
    pad = CONV_K // 2
    t = x.shape[1]
    xp = jnp.pad(x, ((0, 0), (pad, pad), (0, 0)))
    return sum(xp[:, j:j + t] * w[j] for j in range(CONV_K))


def gated_delta_chunked(q, k, v, g, beta, s0):
    b_, t_, h_, _ = q.shape
    dv = v.shape[-1]
    n = t_ // CHUNK

    def chunks(a):
        a = a.reshape((b_, n, CHUNK, h_) + a.shape[3:])
        return jnp.moveaxis(a, (1, 3), (0, 2))

    qc, kc, vc, gc, bc = chunks(q), chunks(k), chunks(v), chunks(g), chunks(beta)
    gcum = jnp.cumsum(gc, axis=-1)
    idx = jnp.arange(CHUNK)
    tril = idx[:, None] >= idx[None, :]
    strict = idx[:, None] > idx[None, :]
    decay = jnp.exp(jnp.where(tril, gcum[..., :, None] - gcum[..., None, :], -jnp.inf))
    kb = kc * bc[..., None]
    lmat = jnp.where(strict, jnp.einsum('nbhid,nbhjd->nbhij', kb, kc) * decay, 0.0)
    amat = lmat + jnp.eye(CHUNK, dtype=jnp.float32)
    rhs = jnp.concatenate([vc * bc[..., None], kb * jnp.exp(gcum)[..., None]], axis=-1)
    sol = lax.linalg.triangular_solve(amat, rhs, left_side=True, lower=True, unit_diagonal=True)
    u, w = sol[..., :dv], sol[..., dv:]
    attn = jnp.einsum('nbhid,nbhjd->nbhij', qc, kc) * decay

    def step(s, inp):
        q_i, k_i, u_i, w_i, a_i, g_i = inp
        v_new = u_i - jnp.einsum('bhcd,bhde->bhce', w_i, s)
        o = (jnp.einsum('bhcd,bhde->bhce', q_i * jnp.exp(g_i)[..., None], s)
             + jnp.einsum('bhij,bhje->bhie', a_i, v_new))
        g_last = g_i[..., -1]
        s = (s * jnp.exp(g_last)[..., None, None]
             + jnp.einsum('bhcd,bhce->bhde', k_i * jnp.exp(g_last[..., None] - g_i)[..., None], v_new))
        return s, o

    s_fin, o = lax.scan(step, s0, (qc, kc, u, w, attn, gcum))
    o = jnp.moveaxis(o, (0, 2), (1, 3)).reshape(b_, t_, h_, dv)
    return o, s_fin


def gated_deltanet(qkv, z, b, a, conv_w, a_log, dt_bias, norm_w, s0):
    bsz, t_, _ = qkv.shape
    f32 = jnp.float32
    qkv = jax.nn.silu(short_conv(qkv, conv_w)).astype(f32)
    q, k, v = jnp.split(qkv, [H_GDN * DK_GDN, 2 * H_GDN * DK_GDN], axis=-1)
    q = l2_normalize(q.reshape(bsz, t_, H_GDN, DK_GDN)) * DK_GDN ** -0.5
    k = l2_normalize(k.reshape(bsz, t_, H_GDN, DK_GDN))
    v = v.reshape(bsz, t_, H_GDN, DV_GDN)
    beta = jax.nn.sigmoid(b.astype(f32)).reshape(bsz, t_, 2, H_GDN)
    g = -jnp.exp(a_log.astype(f32)) * jax.nn.softplus(
        a.astype(f32).reshape(bsz, t_, 2, H_GDN) + dt_bias.astype(f32))
    s0 = s0.astype(f32)
    o_f, s_f = gated_delta_chunked(q, k, v, g[:, :, 0], beta[:, :, 0], s0[:, 0])
    rev = lambda arr: jnp.flip(arr, axis=1)
    o_b, s_b = gated_delta_chunked(rev(q), rev(k), rev(v), rev(g[:, :, 1]), rev(beta[:, :, 1]), s0[:, 1])
    o = o_f + rev(o_b)
    o = rms_norm(o, norm_w) * jax.nn.silu(z.astype(f32).reshape(bsz, t_, H_GDN, DV_GDN))
    return o.reshape(bsz, t_, GDN_W).astype(z.dtype), jnp.stack([s_f, s_b], axis=1)


def block_attention(q, k, v):
    bsz, t_, _, _ = q.shape
    grp = N_Q // N_KV
    nb = t_ // Q_BLOCK
    qb = jnp.moveaxis(q.reshape(bsz, nb, Q_BLOCK, N_KV, grp, HEAD_DIM), 1, 0)
    scale = HEAD_DIM ** -0.5

    def one_block(q_blk):
        s = jnp.einsum('bqkgd,bskd->bkgqs', q_blk, k).astype(jnp.float32) * scale
        p = jax.nn.softmax(s, axis=-1)
        return jnp.einsum('bkgqs,bskd->bqkgd', p.astype(v.dtype), v)

    o = lax.map(one_block, qb)
    return jnp.moveaxis(o, 0, 1).reshape(bsz, t_, ATTN_W)


def mixer(h, lp, rope, ctx_k, ctx_v, s0):
    w_in, conv_w, a_log, dt_bias, gdn_norm_w, q_norm_w, k_norm_w, w_o = lp
    bsz, t_, _ = h.shape
    proj = h @ w_in
    i1 = QKV_W
    i2 = i1 + GDN_W
    i3 = i2 + 2 * H_GDN
    i4 = i3 + 2 * H_GDN
    i5 = i4 + ATTN_W
    i6 = i5 + N_KV * HEAD_DIM
    qkv, z, b, a, q, k, v = jnp.split(proj, [i1, i2, i3, i4, i5, i6], axis=-1)
    o_gdn, s_fin = gated_deltanet(qkv, z, b, a, conv_w, a_log, dt_bias, gdn_norm_w, s0)
    q = rms_norm(q.reshape(bsz, t_, N_Q, HEAD_DIM), q_norm_w)
    k = rms_norm(k.reshape(bsz, t_, N_KV, HEAD_DIM), k_norm_w)
    v = v.reshape(bsz, t_, N_KV, HEAD_DIM)
    if rope is None:
        o_att = block_attention(q, k, v)
    else:
        q_r = apply_axial_rope(q, rope)
        k_r = apply_axial_rope(k, rope)
        k_all = jnp.concatenate([k_r, ctx_k.astype(k.dtype)], axis=1)
        v_all = jnp.concatenate([v, ctx_v.astype(v.dtype)], axis=1)
        o_att = block_attention(q_r, k_all, v_all)
    out = jnp.concatenate([o_gdn, o_att], axis=-1) @ w_o
    return out, k, v, s_fin


def trunk_layer(x, cond, lp, rope, ctx_k, ctx_v, s0):
    (w_ada, b_ada, w_in, conv_w, a_log, dt_bias, gdn_norm_w, q_norm_w, k_norm_w, w_o,
     ln1_g, ln1_b, ln2_g, ln2_b, w_gate_up, w_down) = lp
    m = jax.nn.silu(cond) @ w_ada + b_ada
    if m.ndim == 2:
        m = m[:, None, :]
    sh1, sc1, g1, sh2, sc2, g2 = jnp.split(m, 6, axis=-1)
    h = x * (1.0 + sc1) + sh1
    mix, k, v, s_fin = mixer(h, (w_in, conv_w, a_log, dt_bias, gdn_norm_w, q_norm_w, k_norm_w, w_o),
                             rope, ctx_k, ctx_v, s0)
    x = layer_norm(DN_ALPHA * x + g1 * mix, ln1_g, ln1_b)
    h = x * (1.0 + sc2) + sh2
    gate, up = jnp.split(h @ w_gate_up, 2, axis=-1)
    ffn = (jax.nn.silu(gate) * up) @ w_down
    x = layer_norm(DN_ALPHA * x + g2 * ffn, ln2_g, ln2_b)
    return x, k, v, s_fin


def setup_inputs(seed: int = 0) -> dict:
    key = jax.random.key(seed)
    ks = jax.random.split(key, 24)
    f32 = jnp.float32

    def nrm(k, shape, s):
        return s * jax.random.normal(k, shape, f32)

    dt = jnp.exp(jax.random.uniform(ks[10], (DEPTH, 2, H_GDN), f32, math.log(1e-3), math.log(1e-1)))
    return {
        'x_prompt': nrm(ks[0], (BATCH, SEQ, D_MODEL), 1.0),
        'x_sample': nrm(ks[1], (DEC_BATCH, DEC_SEQ, D_MODEL), 1.0),
        'cache_k': nrm(ks[2], (DEC_BATCH, DEPTH, PAST_LEN, N_KV, HEAD_DIM), 1.0),
        'cache_v': nrm(ks[3], (DEC_BATCH, DEPTH, PAST_LEN, N_KV, HEAD_DIM), 1.0),
        'state_gdn': nrm(ks[4], (DEC_BATCH, DEPTH, 2, H_GDN, DK_GDN, DV_GDN), 0.1),
        'c': nrm(ks[5], (DEC_BATCH, D_MODEL), 1.0),
        'c_ctx': nrm(ks[6], (D_MODEL,), 1.0),
        'w_ada': nrm(ks[7], (DEPTH, D_MODEL, 6 * D_MODEL), 0.5 * D_MODEL ** -0.5),
        'b_ada': nrm(ks[8], (DEPTH, 6 * D_MODEL), 0.01),
        'w_in': nrm(ks[9], (DEPTH, D_MODEL, IN_W), D_MODEL ** -0.5),
        'conv_w': nrm(ks[11], (DEPTH, CONV_K, QKV_W), CONV_K ** -0.5),
        'a_log': jnp.log(jax.random.uniform(ks[12], (DEPTH, 2, H_GDN), f32, 1.0, 16.0)),
        'dt_bias': dt + jnp.log(-jnp.expm1(-dt)),
        'gdn_norm_w': 1.0 + nrm(ks[13], (DEPTH, DV_GDN), 0.1),
        'q_norm_w': 1.0 + nrm(ks[14], (DEPTH, HEAD_DIM), 0.1),
        'k_norm_w': 1.0 + nrm(ks[15], (DEPTH, HEAD_DIM), 0.1),
        'w_o': nrm(ks[16], (DEPTH, MIX_W, D_MODEL), DN_BETA * MIX_W ** -0.5),
        'ln1_g': 1.0 + nrm(ks[17], (DEPTH, D_MODEL), 0.1),
        'ln1_b': nrm(ks[18], (DEPTH, D_MODEL), 0.02),
        'ln2_g': 1.0 + nrm(ks[19], (DEPTH, D_MODEL), 0.1),
        'ln2_b': nrm(ks[20], (DEPTH, D_MODEL), 0.02),
        'w_gate_up': nrm(ks[21], (DEPTH, D_MODEL, 2 * D_FF), D_MODEL ** -0.5),
        'w_down': nrm(ks[22], (DEPTH, D_FF, D_MODEL), DN_BETA * D_FF ** -0.5),
    }


def reference(x_prompt, x_sample, cache_k, cache_v, state_gdn, c, c_ctx, w_ada, b_ada, w_in,
              conv_w, a_log, dt_bias, gdn_norm_w, q_norm_w, k_norm_w, w_o, ln1_g, ln1_b,
              ln2_g, ln2_b, w_gate_up, w_down):
    rope = axial_rope_tables(x_sample.shape[1])
    s_zero = jnp.zeros((x_prompt.shape[0], 2, H_GDN, DK_GDN, DV_GDN), jnp.float32)
    xp = x_prompt
    xs = x_sample
    ks_out, vs_out, ss_out = [], [], []
    for l in range(DEPTH):
        lp = (w_ada[l], b_ada[l], w_in[l], conv_w[l], a_log[l], dt_bias[l], gdn_norm_w[l],
              q_norm_w[l], k_norm_w[l], w_o[l], ln1_g[l], ln1_b[l], ln2_g[l], ln2_b[l],
              w_gate_up[l], w_down[l])
        xp, k_ctx, v_ctx, s_ctx = trunk_layer(xp, c_ctx, lp, None, None, None, s_zero)
        ks_out.append(k_ctx)
        vs_out.append(v_ctx)
        ss_out.append(s_ctx.astype(x_prompt.dtype))
        xs, _, _, _ = trunk_layer(xs, c, lp, rope, cache_k[:, l], cache_v[:, l], state_gdn[:, l])
    new_cache_k = jnp.stack(ks_out, axis=1)
    new_cache_v = jnp.stack(vs_out, axis=1)
    new_state_gdn = jnp.stack(ss_out, axis=1)
    return (xp, xs, new_cache_k, new_cache_v, new_state_gdn)
```

```python
import functools
import math

import jax
import jax.numpy as jnp
from jax import lax
from jax.experimental import pallas as pl
from jax.experimental.pallas import tpu as pltpu

D_MODEL = 2048
DEPTH = 2
GRID_W = 64
DK = 128
H_GDN = 8
GDN_W = H_GDN * DK
HEAD_DIM = 128
N_Q = 8
N_KV = 2
GRP = N_Q // N_KV
ATTN_W = N_Q * HEAD_DIM
KV_W = N_KV * HEAD_DIM
CONV_K = 5
D_FF = 5632
ROPE_BASE = 10000.0
RMS_EPS = 1e-6
LN_EPS = 1e-5
DN_ALPHA = (2 * DEPTH) ** 0.25
QKV_W = 3 * GDN_W
IN_W = QKV_W + GDN_W + 4 * H_GDN + ATTN_W + 2 * KV_W

LANE = 128
GDN_CHUNK = 128
COL_Z = QKV_W
COL_Q = COL_Z + GDN_W
COL_K = COL_Q + ATTN_W
COL_V = COL_K + KV_W
COL_BA = COL_V + KV_W
IN_W_PAD = COL_BA + LANE

TM = 512
TN_IN = 1920
TN_FF = 512
TK_FF = 512
TN_ADA = 1024
VMEM_LIMIT = 56 * 1024 * 1024

F32 = jnp.float32
BF16 = jnp.bfloat16


def _cparams(sem):
    return pltpu.CompilerParams(dimension_semantics=sem, vmem_limit_bytes=VMEM_LIMIT)


def _dot(a, b):
    return jnp.dot(a, b, preferred_element_type=F32)


def _dot_f32(a, b):
    return jnp.dot(a, b, preferred_element_type=F32, precision=lax.Precision.HIGHEST)


def _dot_nt(a, b):
    return lax.dot_general(a, b, (((1,), (1,)), ((), ())), preferred_element_type=F32)


def _dot_tn(a, b):
    return lax.dot_general(a, b, (((0,), (0,)), ((), ())), preferred_element_type=F32)


def _sigmoid(x):
    return 1.0 / (1.0 + jnp.exp(-x))


def _silu(x):
    return x * _sigmoid(x)


def _layer_norm(y, g, b):
    mu = jnp.mean(y, axis=-1, keepdims=True)
    yc = y - mu
    var = jnp.mean(yc * yc, axis=-1, keepdims=True)
    return yc * lax.rsqrt(var + LN_EPS) * g + b


def _rms_norm(x, w):
    return x * lax.rsqrt(jnp.mean(x * x, axis=-1, keepdims=True) + RMS_EPS) * w


def _ada_kernel(c_ref, w_ref, b_ref, o_ref):
    c = c_ref[...]
    o_ref[...] = _dot(_silu(c).astype(BF16), w_ref[...].astype(BF16)) + b_ref[...]


def _ada(cond8, w_ada, b_ada):
    n = w_ada.shape[-1]
    return pl.pallas_call(
        _ada_kernel,
        grid=(DEPTH, n // TN_ADA),
        in_specs=[
            pl.BlockSpec((8, D_MODEL), lambda l, j: (0, 0)),
            pl.BlockSpec((None, D_MODEL, TN_ADA), lambda l, j: (l, 0, j)),
            pl.BlockSpec((None, 1, TN_ADA), lambda l, j: (l, 0, j)),
        ],
        out_specs=pl.BlockSpec((None, 8, TN_ADA), lambda l, j: (l, 0, j)),
        out_shape=jax.ShapeDtypeStruct((DEPTH, 8, n), F32),
        compiler_params=_cparams(("arbitrary", "arbitrary")),
        name="ada",
    )(cond8, w_ada, b_ada)


def _inproj_kernel(x_ref, m_ref, w_ref, o_ref, h_ref):
    @pl.when(pl.program_id(1) == 0)
    def _():
        h_ref[...] = (x_ref[...] * (1.0 + m_ref[1:2, :]) + m_ref[0:1, :]).astype(BF16)

    o_ref[...] = _dot(h_ref[...], w_ref[...])


def _in_proj(x, mod, w, tiles_per_group):
    n = x.shape[0]
    return pl.pallas_call(
        _inproj_kernel,
        grid=(n // TM, IN_W_PAD // TN_IN),
        in_specs=[
            pl.BlockSpec((TM, D_MODEL), lambda i, j: (i, 0)),
            pl.BlockSpec((None, 6, D_MODEL), lambda i, j: (i // tiles_per_group, 0, 0)),
            pl.BlockSpec((D_MODEL, TN_IN), lambda i, j: (0, j)),
        ],
        out_specs=pl.BlockSpec((TM, TN_IN), lambda i, j: (i, j)),
        out_shape=jax.ShapeDtypeStruct((n, IN_W_PAD), F32),
        scratch_shapes=[pltpu.VMEM((TM, D_MODEL), BF16)],
        compiler_params=_cparams(("parallel", "arbitrary")),
        name="in_proj",
    )(x, mod, w)


def _conv_silu(x, w_ref):
    t = x.shape[0]
    row = lax.broadcasted_iota(jnp.int32, x.shape, 0)
    acc = None
    for j in range(CONV_K):
        s = CONV_K // 2 - j
        if s == 0:
            xs = x
        else:
            xs = pltpu.roll(x, shift=s % t, axis=0)
            valid = (row >= s) if s > 0 else (row < t + s)
            xs = jnp.where(valid, xs, 0.0)
        term = xs * w_ref[j:j + 1, :]
        acc = term if acc is None else acc + term
    return _silu(acc)


def _gdn_chunk(qs, ks, vs, gs, bs, os, r0, d, s_state):
    c = GDN_CHUNK
    rows = pl.ds(r0, c)
    qc = qs[rows, :]
    kc = ks[rows, :]
    vc = vs[rows, :]
    g = gs[d, rows, :]
    beta = bs[d, rows, :]
    row = lax.broadcasted_iota(jnp.int32, (c, c), 0)
    col = lax.broadcasted_iota(jnp.int32, (c, c), 1)
    if d == 0:
        incl, strict = row >= col, row > col
    else:
        incl, strict = row <= col, row < col
    tri = incl.astype(F32)
    gcum = jnp.dot(tri, g, preferred_element_type=F32, precision=lax.Precision.HIGHEST)
    dm = gcum - gcum.T
    decay = jnp.where(incl, jnp.exp(jnp.where(incl, dm, 0.0)), 0.0)
    kb = kc * beta
    kcb = kc.astype(BF16)
    lm = jnp.where(strict, _dot_nt(kb.astype(BF16), kcb) * decay, 0.0)
    attn = _dot_nt(qc.astype(BF16), kcb) * decay
    mk = -lm
    e = mk
    for _ in range(int(math.log2(c)) - 1):
        mk = _dot_f32(mk, mk)
        e = e + mk + _dot_f32(e, mk)
    egc = jnp.exp(gcum)
    rhs = jnp.concatenate([vc * beta, kb * egc], axis=1)
    sol = rhs + _dot(e.astype(BF16), rhs.astype(BF16))
    u, w = sol[:, :DK], sol[:, DK:]
    sb = s_state.astype(BF16)
    v_new = u - _dot(w.astype(BF16), sb)
    o = _dot((qc * egc).astype(BF16), sb) + _dot(attn.astype(BF16), v_new.astype(BF16))
    os[rows, :] += o
    g_last = gcum[c - 1:c, :] if d == 0 else gcum[0:1, :]
    kdec = kc * jnp.exp(g_last - gcum)
    return s_state * jnp.exp(g_last) + _dot_tn(kdec.astype(BF16), v_new.astype(BF16))


def _gdn_kernel(alog_ref, dtb_ref, q_ref, k_ref, v_ref, z_ref, ba_ref, cwq_ref, cwk_ref, cwv_ref,
                nw_ref, *rest, has_s0, has_sout):
    rest = list(rest)
    s0_ref = rest.pop(0) if has_s0 else None
    o_ref = rest.pop(0)
    sout_ref = rest.pop(0) if has_sout else None
    qs, ks, vs, gs, bs, os = rest
    h = pl.program_id(1)
    t = q_ref.shape[0]

    q = _conv_silu(q_ref[...], cwq_ref)
    qs[...] = q * lax.rsqrt(jnp.sum(q * q, axis=-1, keepdims=True) + RMS_EPS) * (DK ** -0.5)
    k = _conv_silu(k_ref[...], cwk_ref)
    ks[...] = k * lax.rsqrt(jnp.sum(k * k, axis=-1, keepdims=True) + RMS_EPS)
    vs[...] = _conv_silu(v_ref[...], cwv_ref)

    ba = ba_ref[...]
    lane = lax.broadcasted_iota(jnp.int32, ba.shape, 1)
    for d in range(2):
        bcol = jnp.sum(jnp.where(lane == d * H_GDN + h, ba, 0.0), axis=-1, keepdims=True)
        acol = jnp.sum(jnp.where(lane == 2 * H_GDN + d * H_GDN + h, ba, 0.0), axis=-1, keepdims=True)
        rate = -jnp.exp(jnp.full((1, LANE), alog_ref[d, h], F32))
        x = acol + dtb_ref[d, h]
        softplus = jnp.maximum(x, 0.0) + jnp.log(1.0 + jnp.exp(-jnp.abs(x)))
        gs[d] = jnp.broadcast_to(softplus, (t, LANE)) * rate
        bs[d] = jnp.broadcast_to(_sigmoid(bcol), (t, LANE))
    os[...] = jnp.zeros_like(os)

    n_chunks = t // GDN_CHUNK
    if has_s0:
        s_init = (s0_ref[0], s0_ref[1])
    else:
        s_init = (jnp.zeros((DK, DK), F32), jnp.zeros((DK, DK), F32))

    def body(i, carry):
        s_f, s_b = carry
        r_f = pl.multiple_of(i * GDN_CHUNK, GDN_CHUNK)
        r_b = pl.multiple_of((n_chunks - 1 - i) * GDN_CHUNK, GDN_CHUNK)
        s_f = _gdn_chunk(qs, ks, vs, gs, bs, os, r_f, 0, s_f)
        s_b = _gdn_chunk(qs, ks, vs, gs, bs, os, r_b, 1, s_b)
        return s_f, s_b

    s_f, s_b = lax.fori_loop(0, n_chunks, body, s_init)
    if has_sout:
        sout_ref[0] = s_f
        sout_ref[1] = s_b

    o = os[...]
    z = z_ref[...]
    o_ref[...] = (_rms_norm(o, nw_ref[...]) * _silu(z)).astype(o_ref.dtype)


def _gdn(proj, conv_w, a_log, dt_bias, norm_w, s0, l, bsz, t, want_state):
    has_s0 = s0 is not None
    col = lambda off: (lambda b, h: (b, off + h))
    smem = pl.BlockSpec(memory_space=pltpu.SMEM)
    in_specs = [
        smem, smem,
        pl.BlockSpec((t, DK), col(0)),
        pl.BlockSpec((t, DK), col(H_GDN)),
        pl.BlockSpec((t, DK), col(2 * H_GDN)),
        pl.BlockSpec((t, DK), col(COL_Z // LANE)),
        pl.BlockSpec((t, LANE), lambda b, h: (b, COL_BA // LANE)),
        pl.BlockSpec((CONV_K, DK), lambda b, h: (0, h)),
        pl.BlockSpec((CONV_K, DK), lambda b, h: (0, H_GDN + h)),
        pl.BlockSpec((CONV_K, DK), lambda b, h: (0, 2 * H_GDN + h)),
        pl.BlockSpec((1, DK), lambda b, h: (0, 0)),
    ]
    args = [a_log, dt_bias, proj, proj, proj, proj, proj, conv_w, conv_w, conv_w, norm_w]
    if has_s0:
        in_specs.append(pl.BlockSpec((None, None, 2, None, DK, DK), lambda b, h: (b, l, 0, h, 0, 0)))
        args.append(s0)
    out_specs = [pl.BlockSpec((t, DK), lambda b, h: (b, h))]
    out_shape = [jax.ShapeDtypeStruct((bsz * t, GDN_W), BF16)]
    if want_state:
        out_specs.append(pl.BlockSpec((None, 2, None, DK, DK), lambda b, h: (b, 0, h, 0, 0)))
        out_shape.append(jax.ShapeDtypeStruct((bsz, 2, H_GDN, DK, DK), F32))
    return pl.pallas_call(
        functools.partial(_gdn_kernel, has_s0=has_s0, has_sout=want_state),
        grid=(bsz, H_GDN),
        in_specs=in_specs,
        out_specs=out_specs,
        out_shape=out_shape,
        scratch_shapes=[
            pltpu.VMEM((t, DK), F32), pltpu.VMEM((t, DK), F32), pltpu.VMEM((t, DK), F32),
            pltpu.VMEM((2, t, LANE), F32), pltpu.VMEM((2, t, LANE), F32), pltpu.VMEM((t, DK), F32),
        ],
        compiler_params=_cparams(("parallel", "arbitrary")),
        name="gdn_lat" if has_s0 else "gdn_ctx",
    )(*args)


def _rope(x, cos, sin):
    lane = lax.broadcasted_iota(jnp.int32, x.shape, 1)
    quarter = HEAD_DIM // 4
    partner = jnp.where((lane % (2 * quarter)) < quarter,
                        pltpu.roll(x, shift=HEAD_DIM - quarter, axis=1),
                        pltpu.roll(x, shift=quarter, axis=1))
    return x * cos + partner * sin


def _softmax_pv(score_blocks, value_blocks):
    m = None
    for s in score_blocks:
        bm = jnp.max(s, axis=-1, keepdims=True)
        m = bm if m is None else jnp.maximum(m, bm)
    den, acc = None, None
    for s, v in zip(score_blocks, value_blocks):
        p = jnp.exp(s - m)
        ps = jnp.sum(p, axis=-1, keepdims=True)
        pv = _dot(p.astype(BF16), v)
        den = ps if den is None else den + ps
        acc = pv if acc is None else acc + pv
    return acc / den


def _attn_ctx_kernel(q_ref, k_ref, v_ref, qw_ref, kw_ref, o_ref, kc_ref, vc_ref):
    scale = HEAD_DIM ** -0.5
    vc_ref[...] = v_ref[...]
    for g in range(N_KV):
        gl = slice(g * HEAD_DIM, (g + 1) * HEAD_DIM)
        kg = _rms_norm(k_ref[:, gl], kw_ref[...])
        kc_ref[:, gl] = kg
        kgb = kg.astype(BF16)
        vgb = v_ref[:, gl].astype(BF16)
        for hh in range(GRP):
            hl = slice((g * GRP + hh) * HEAD_DIM, (g * GRP + hh + 1) * HEAD_DIM)
            qh = _rms_norm(q_ref[:, hl], qw_ref[...]).astype(BF16)
            s = _dot_nt(qh, kgb) * scale
            o_ref[:, hl] = _softmax_pv([s], [vgb]).astype(o_ref.dtype)


def _attn_ctx(proj, q_norm_w, k_norm_w, bsz, t):
    return pl.pallas_call(
        _attn_ctx_kernel,
        grid=(bsz,),
        in_specs=[
            pl.BlockSpec((t, ATTN_W), lambda b: (b, COL_Q // ATTN_W)),
            pl.BlockSpec((t, KV_W), lambda b: (b, COL_K // KV_W)),
            pl.BlockSpec((t, KV_W), lambda b: (b, COL_V // KV_W)),
            pl.BlockSpec((1, HEAD_DIM), lambda b: (0, 0)),
            pl.BlockSpec((1, HEAD_DIM), lambda b: (0, 0)),
        ],
        out_specs=[
            pl.BlockSpec((t, ATTN_W), lambda b: (b, 0)),
            pl.BlockSpec((None, t, KV_W), lambda b: (b, 0, 0)),
            pl.BlockSpec((None, t, KV_W), lambda b: (b, 0, 0)),
        ],
        out_shape=[
            jax.ShapeDtypeStruct((bsz * t, ATTN_W), BF16),
            jax.ShapeDtypeStruct((bsz, t, KV_W), F32),
            jax.ShapeDtypeStruct((bsz, t, KV_W), F32),
        ],
        compiler_params=_cparams(("parallel",)),
        name="attn_ctx",
    )(proj, proj, proj, q_norm_w, k_norm_w)


def _attn_lat_kernel(q_ref, k_ref, v_ref, ck_ref, cv_ref, qw_ref, kw_ref, cq_ref, sq_ref, ck_cos_ref, ck_sin_ref,
                     o_ref):
    scale = HEAD_DIM ** -0.5
    for g in range(N_KV):
        gl = slice(g * HEAD_DIM, (g + 1) * HEAD_DIM)
        kg = _rope(_rms_norm(k_ref[:, gl], kw_ref[...]), ck_cos_ref[...], ck_sin_ref[...]).astype(BF16)
        vg = v_ref[:, gl].astype(BF16)
        ckg = ck_ref[:, gl].astype(BF16)
        cvg = cv_ref[:, gl].astype(BF16)
        for hh in range(GRP):
            hl = slice((g * GRP + hh) * HEAD_DIM, (g * GRP + hh + 1) * HEAD_DIM)
            qh = _rope(_rms_norm(q_ref[:, hl], qw_ref[...]), cq_ref[...], sq_ref[...]).astype(BF16)
            s1 = _dot_nt(qh, kg) * scale
            s2 = _dot_nt(qh, ckg) * scale
            o_ref[:, hl] = _softmax_pv([s1, s2], [vg, cvg]).astype(o_ref.dtype)


def _attn_lat(proj, cache_k, cache_v, q_norm_w, k_norm_w, cos_t, sin_t, l, bsz, t, tq):
    past = cache_k.shape[2]
    nq = t // tq
    return pl.pallas_call(
        _attn_lat_kernel,
        grid=(bsz, nq),
        in_specs=[
            pl.BlockSpec((tq, ATTN_W), lambda b, i: (b * nq + i, COL_Q // ATTN_W)),
            pl.BlockSpec((t, KV_W), lambda b, i: (b, COL_K // KV_W)),
            pl.BlockSpec((t, KV_W), lambda b, i: (b, COL_V // KV_W)),
            pl.BlockSpec((None, None, past, KV_W), lambda b, i: (b, l, 0, 0)),
            pl.BlockSpec((None, None, past, KV_W), lambda b, i: (b, l, 0, 0)),
            pl.BlockSpec((1, HEAD_DIM), lambda b, i: (0, 0)),
            pl.BlockSpec((1, HEAD_DIM), lambda b, i: (0, 0)),
            pl.BlockSpec((tq, HEAD_DIM), lambda b, i: (i, 0)),
            pl.BlockSpec((tq, HEAD_DIM), lambda b, i: (i, 0)),
            pl.BlockSpec((t, HEAD_DIM), lambda b, i: (0, 0)),
            pl.BlockSpec((t, HEAD_DIM), lambda b, i: (0, 0)),
        ],
        out_specs=pl.BlockSpec((tq, ATTN_W), lambda b, i: (b * nq + i, 0)),
        out_shape=jax.ShapeDtypeStruct((bsz * t, ATTN_W), BF16),
        compiler_params=_cparams(("parallel", "arbitrary")),
        name="attn_lat",
    )(proj, proj, proj, cache_k, cache_v, q_norm_w, k_norm_w, cos_t, sin_t, cos_t, sin_t)


def _rope_lane_tables(n_tok):
    rows = n_tok // GRID_W
    row = jnp.repeat(jnp.arange(rows), GRID_W).astype(F32)
    colp = jnp.tile(jnp.arange(GRID_W), rows).astype(F32)
    half = HEAD_DIM // 2
    inv_freq = ROPE_BASE ** (-jnp.arange(0, half, 2, dtype=F32) / half)
    ang_r = row[:, None] * inv_freq
    ang_c = colp[:, None] * inv_freq
    cos_t = jnp.concatenate([jnp.cos(ang_r), jnp.cos(ang_r), jnp.cos(ang_c), jnp.cos(ang_c)], axis=-1)
    sin_t = jnp.concatenate([-jnp.sin(ang_r), jnp.sin(ang_r), -jnp.sin(ang_c), jnp.sin(ang_c)], axis=-1)
    return cos_t, sin_t


def _outproj_kernel(og_ref, oa_ref, wg_ref, wa_ref, x_ref, m_ref, g_ref, b_ref, o_ref):
    mix = _dot(og_ref[...], wg_ref[...]) + _dot(oa_ref[...], wa_ref[...])
    y = DN_ALPHA * x_ref[...] + m_ref[2:3, :] * mix
    o_ref[...] = _layer_norm(y, g_ref[...], b_ref[...])


def _out_proj(o_gdn, o_att, w_o, x, mod, ln_g, ln_b, tiles_per_group):
    n = x.shape[0]
    row = lambda i: (i, 0)
    const = lambda i: (0, 0)
    return pl.pallas_call(
        _outproj_kernel,
        grid=(n // TM,),
        in_specs=[
            pl.BlockSpec((TM, GDN_W), row),
            pl.BlockSpec((TM, ATTN_W), row),
            pl.BlockSpec((GDN_W, D_MODEL), lambda i: (0, 0)),
            pl.BlockSpec((ATTN_W, D_MODEL), lambda i: (1, 0)),
            pl.BlockSpec((TM, D_MODEL), row),
            pl.BlockSpec((None, 6, D_MODEL), lambda i: (i // tiles_per_group, 0, 0)),
            pl.BlockSpec((1, D_MODEL), const),
            pl.BlockSpec((1, D_MODEL), const),
        ],
        out_specs=pl.BlockSpec((TM, D_MODEL), row),
        out_shape=jax.ShapeDtypeStruct((n, D_MODEL), F32),
        compiler_params=_cparams(("parallel",)),
        name="out_proj",
    )(o_gdn, o_att, w_o, w_o, x, mod, ln_g, ln_b)


def _ffn_up_kernel(x_ref, m_ref, wg_ref, wu_ref, o_ref, h_ref):
    @pl.when(pl.program_id(1) == 0)
    def _():
        h_ref[...] = (x_ref[...] * (1.0 + m_ref[4:5, :]) + m_ref[3:4, :]).astype(BF16)

    h = h_ref[...]
    gate = _dot(h, wg_ref[...])
    up = _dot(h, wu_ref[...])
    o_ref[...] = (_silu(gate) * up).astype(o_ref.dtype)


def _ffn_up(x, mod, w_gu, tiles_per_group):
    n = x.shape[0]
    nj = D_FF // TN_FF
    return pl.pallas_call(
        _ffn_up_kernel,
        grid=(n // TM, nj),
        in_specs=[
            pl.BlockSpec((TM, D_MODEL), lambda i, j: (i, 0)),
            pl.BlockSpec((None, 6, D_MODEL), lambda i, j: (i // tiles_per_group, 0, 0)),
            pl.BlockSpec((D_MODEL, TN_FF), lambda i, j: (0, j)),
            pl.BlockSpec((D_MODEL, TN_FF), lambda i, j: (0, nj + j)),
        ],
        out_specs=pl.BlockSpec((TM, TN_FF), lambda i, j: (i, j)),
        out_shape=jax.ShapeDtypeStruct((n, D_FF), BF16),
        scratch_shapes=[pltpu.VMEM((TM, D_MODEL), BF16)],
        compiler_params=_cparams(("parallel", "arbitrary")),
        name="ffn_up",
    )(x, mod, w_gu, w_gu)


def _ffn_down_kernel(a_ref, w_ref, x_ref, m_ref, g_ref, b_ref, o_ref, acc_ref):
    k = pl.program_id(1)

    @pl.when(k == 0)
    def _():
        acc_ref[...] = jnp.zeros_like(acc_ref)

    acc_ref[...] += _dot(a_ref[...], w_ref[...])

    @pl.when(k == pl.num_programs(1) - 1)
    def _():
        y = DN_ALPHA * x_ref[...] + m_ref[5:6, :] * acc_ref[...]
        o_ref[...] = _layer_norm(y, g_ref[...], b_ref[...])


def _ffn_down(act, w_down, x, mod, ln_g, ln_b, tiles_per_group):
    n = x.shape[0]
    return pl.pallas_call(
        _ffn_down_kernel,
        grid=(n // TM, D_FF // TK_FF),
        in_specs=[
            pl.BlockSpec((TM, TK_FF), lambda i, k: (i, k)),
            pl.BlockSpec((TK_FF, D_MODEL), lambda i, k: (k, 0)),
            pl.BlockSpec((TM, D_MODEL), lambda i, k: (i, 0)),
            pl.BlockSpec((None, 6, D_MODEL), lambda i, k: (i // tiles_per_group, 0, 0)),
            pl.BlockSpec((1, D_MODEL), lambda i, k: (0, 0)),
            pl.BlockSpec((1, D_MODEL), lambda i, k: (0, 0)),
        ],
        out_specs=pl.BlockSpec((TM, D_MODEL), lambda i, k: (i, 0)),
        out_shape=jax.ShapeDtypeStruct((n, D_MODEL), F32),
        scratch_shapes=[pltpu.VMEM((TM, D_MODEL), F32)],
        compiler_params=_cparams(("parallel", "arbitrary")),
        name="ffn_down",
    )(act, w_down, x, mod, ln_g, ln_b)


def _trunk_layer(x, mod, lw, l, bsz, t, latent):
    assert t % TM == 0 or TM % t == 0
    tiles_per_group = max(t // TM, 1) if latent is not None else x.shape[0] // TM
    proj = _in_proj(x, mod, lw["w_in"], tiles_per_group)
    if latent is None:
        o_gdn, state = _gdn(proj, lw["conv_w"], lw["a_log"], lw["dt_bias"], lw["gdn_norm_w"], None, l, bsz, t, True)
        o_att, k_c, v_c = _attn_ctx(proj, lw["q_norm_w"], lw["k_norm_w"], bsz, t)
    else:
        cache_k, cache_v, state_gdn, cos_t, sin_t = latent
        (o_gdn,) = _gdn(proj, lw["conv_w"], lw["a_log"], lw["dt_bias"], lw["gdn_norm_w"], state_gdn, l, bsz, t, False)
        o_att = _attn_lat(proj, cache_k, cache_v, lw["q_norm_w"], lw["k_norm_w"], cos_t, sin_t, l, bsz, t, 256)
        state = k_c = v_c = None
    x1 = _out_proj(o_gdn, o_att, lw["w_o"], x, mod, lw["ln1_g"], lw["ln1_b"], tiles_per_group)
    act = _ffn_up(x1, mod, lw["w_gate_up"], tiles_per_group)
    x2 = _ffn_down(act, lw["w_down"], x1, mod, lw["ln2_g"], lw["ln2_b"], tiles_per_group)
    return x2, k_c, v_c, state


def kernel(x_prompt, x_sample, cache_k, cache_v, state_gdn, c, c_ctx, w_ada, b_ada, w_in, conv_w, a_log, dt_bias,
           gdn_norm_w, q_norm_w, k_norm_w, w_o, ln1_g, ln1_b, ln2_g, ln2_b, w_gate_up, w_down):
    bsz, seq, _ = x_prompt.shape
    dbsz, dseq, _ = x_sample.shape
    past = cache_k.shape[2]

    i_b = QKV_W + GDN_W
    i_q = i_b + 4 * H_GDN
    w_in_r = jnp.concatenate(
        [w_in[:, :, :i_b], w_in[:, :, i_q:], w_in[:, :, i_b:i_q],
         jnp.zeros((DEPTH, D_MODEL, LANE - 4 * H_GDN), w_in.dtype)], axis=-1).astype(BF16)
    w_o_b = w_o.astype(BF16)
    w_gu_b = w_gate_up.astype(BF16)
    w_down_b = w_down.astype(BF16)

    cond8 = jnp.concatenate([c_ctx[None, :], c, jnp.zeros((8 - 1 - dbsz, D_MODEL), F32)], axis=0)
    mod = _ada(cond8, w_ada, b_ada.reshape(DEPTH, 1, 6 * D_MODEL)).reshape(DEPTH, 8, 6, D_MODEL)

    cos_t, sin_t = _rope_lane_tables(dseq)
    ck = cache_k.reshape(dbsz, DEPTH, past, KV_W)
    cv = cache_v.reshape(dbsz, DEPTH, past, KV_W)

    xp = x_prompt.reshape(bsz * seq, D_MODEL)
    xs = x_sample.reshape(dbsz * dseq, D_MODEL)
    ks_out, vs_out, ss_out = [], [], []
    for l in range(DEPTH):
        lw = {
            "w_in": w_in_r[l], "conv_w": conv_w[l], "a_log": a_log[l], "dt_bias": dt_bias[l],
            "gdn_norm_w": gdn_norm_w[l].reshape(1, DK), "q_norm_w": q_norm_w[l].reshape(1, HEAD_DIM),
            "k_norm_w": k_norm_w[l].reshape(1, HEAD_DIM), "w_o": w_o_b[l],
            "ln1_g": ln1_g[l].reshape(1, D_MODEL), "ln1_b": ln1_b[l].reshape(1, D_MODEL),
            "ln2_g": ln2_g[l].reshape(1, D_MODEL), "ln2_b": ln2_b[l].reshape(1, D_MODEL),
            "w_gate_up": w_gu_b[l], "w_down": w_down_b[l],
        }
        xp, k_c, v_c, s_c = _trunk_layer(xp, mod[l, 0:1], lw, l, bsz, seq, None)
        ks_out.append(k_c.reshape(bsz, seq, N_KV, HEAD_DIM))
        vs_out.append(v_c.reshape(bsz, seq, N_KV, HEAD_DIM))
        ss_out.append(s_c)
        xs, _, _, _ = _trunk_layer(xs, mod[l, 1:1 + dbsz], lw, l, dbsz, dseq, (ck, cv, state_gdn, cos_t, sin_t))
    return (xp.reshape(bsz, seq, D_MODEL), xs.reshape(dbsz, dseq, D_MODEL),
            jnp.stack(ks_out, axis=1), jnp.stack(vs_out, axis=1), jnp.stack(ss_out, axis=1))
```

```python
import functools
import math

import jax
import jax.numpy as jnp
from jax import lax
from jax.experimental import pallas as pl
from jax.experimental.pallas import tpu as pltpu

D_MODEL = 2048
DEPTH = 2
GRID_W = 64
DK = 128
H_GDN = 8
GDN_W = H_GDN * DK
HEAD_DIM = 128
N_Q = 8
N_KV = 2
GRP = N_Q // N_KV
ATTN_W = N_Q * HEAD_DIM
KV_W = N_KV * HEAD_DIM
CONV_K = 5
D_FF = 5632
ROPE_BASE = 10000.0
RMS_EPS = 1e-6
LN_EPS = 1e-5
DN_ALPHA = (2 * DEPTH) ** 0.25
QKV_W = 3 * GDN_W
IN_W = QKV_W + GDN_W + 4 * H_GDN + ATTN_W + 2 * KV_W

LANE = 128
GDN_CHUNK = 128
GDN_HB = 2
GDN_CG = 2
COL_Z = QKV_W
COL_Q = COL_Z + GDN_W
COL_K = COL_Q + ATTN_W
COL_V = COL_K + KV_W
COL_BA = COL_V + KV_W
IN_W_PAD = COL_BA + LANE

TM = 512
TN_IN = 1920
TN_FF = 512
TK_FF = 512
TN_ADA = 1024
VMEM_LIMIT = 56 * 1024 * 1024

F32 = jnp.float32
BF16 = jnp.bfloat16


def _cparams(sem):
    return pltpu.CompilerParams(dimension_semantics=sem, vmem_limit_bytes=VMEM_LIMIT)


def _dot(a, b):
    return jnp.dot(a, b, preferred_element_type=F32)


def _dot_nt(a, b):
    return lax.dot_general(a, b, (((1,), (1,)), ((), ())), preferred_element_type=F32)


def _dot_tn(a, b):
    return lax.dot_general(a, b, (((0,), (0,)), ((), ())), preferred_element_type=F32)


def _sigmoid(x):
    return 1.0 / (1.0 + jnp.exp(-x))


def _silu(x):
    return x * _sigmoid(x)


def _layer_norm(y, g, b):
    mu = jnp.mean(y, axis=-1, keepdims=True)
    yc = y - mu
    var = jnp.mean(yc * yc, axis=-1, keepdims=True)
    return yc * lax.rsqrt(var + LN_EPS) * g + b


def _rms_norm(x, w):
    return x * lax.rsqrt(jnp.mean(x * x, axis=-1, keepdims=True) + RMS_EPS) * w


def _ada_kernel(c_ref, w_ref, b_ref, o_ref):
    c = c_ref[...]
    o_ref[...] = _dot(_silu(c).astype(BF16), w_ref[...].astype(BF16)) + b_ref[...]


def _ada(cond8, w_ada, b_ada):
    n = w_ada.shape[-1]
    return pl.pallas_call(
        _ada_kernel,
        grid=(DEPTH, n // TN_ADA),
        in_specs=[
            pl.BlockSpec((8, D_MODEL), lambda l, j: (0, 0)),
            pl.BlockSpec((None, D_MODEL, TN_ADA), lambda l, j: (l, 0, j)),
            pl.BlockSpec((None, 1, TN_ADA), lambda l, j: (l, 0, j)),
        ],
        out_specs=pl.BlockSpec((None, 8, TN_ADA), lambda l, j: (l, 0, j)),
        out_shape=jax.ShapeDtypeStruct((DEPTH, 8, n), F32),
        compiler_params=_cparams(("arbitrary", "arbitrary")),
        name="ada",
    )(cond8, w_ada, b_ada)


def _inproj_kernel(x_ref, m_ref, w_ref, o_ref, h_ref):
    @pl.when(pl.program_id(1) == 0)
    def _():
        h_ref[...] = (x_ref[...] * (1.0 + m_ref[1:2, :]) + m_ref[0:1, :]).astype(BF16)

    o_ref[...] = _dot(h_ref[...], w_ref[...])


def _in_proj(x, mod, w, tiles_per_group):
    n = x.shape[0]
    return pl.pallas_call(
        _inproj_kernel,
        grid=(n // TM, IN_W_PAD // TN_IN),
        in_specs=[
            pl.BlockSpec((TM, D_MODEL), lambda i, j: (i, 0)),
            pl.BlockSpec((None, 6, D_MODEL), lambda i, j: (i // tiles_per_group, 0, 0)),
            pl.BlockSpec((D_MODEL, TN_IN), lambda i, j: (0, j)),
        ],
        out_specs=pl.BlockSpec((TM, TN_IN), lambda i, j: (i, j)),
        out_shape=jax.ShapeDtypeStruct((n, IN_W_PAD), F32),
        scratch_shapes=[pltpu.VMEM((TM, D_MODEL), BF16)],
        compiler_params=_cparams(("parallel", "arbitrary")),
        name="in_proj",
    )(x, mod, w)


def _conv_silu(x, w_ref, cols):
    t = x.shape[0]
    row = lax.broadcasted_iota(jnp.int32, x.shape, 0)
    acc = None
    for j in range(CONV_K):
        s = CONV_K // 2 - j
        if s == 0:
            xs = x
        else:
            xs = pltpu.roll(x, shift=s % t, axis=0)
            valid = (row >= s) if s > 0 else (row < t + s)
            xs = jnp.where(valid, xs, 0.0)
        term = xs * w_ref[j:j + 1, cols]
        acc = term if acc is None else acc + term
    return _silu(acc)


def _seg_cumsum(x, reverse):
    t = x.shape[0]
    pos = lax.broadcasted_iota(jnp.int32, x.shape, 0) % GDN_CHUNK
    s = 1
    while s < GDN_CHUNK:
        if reverse:
            x = x + jnp.where(pos < GDN_CHUNK - s, pltpu.roll(x, shift=t - s, axis=0), 0.0)
        else:
            x = x + jnp.where(pos >= s, pltpu.roll(x, shift=s, axis=0), 0.0)
        s *= 2
    return x


def _unit_tri_inverses_minus_eye(lms):
    c = lms[0].shape[0]
    row = lax.broadcasted_iota(jnp.int32, (c, c), 0)
    col = lax.broadcasted_iota(jnp.int32, (c, c), 1)
    rc = row ^ col
    es = None
    for lvl in range(int(math.log2(c))):
        css = [jnp.where((rc >> lvl) == 1, lm, 0.0) for lm in lms]
        if es is None:
            es = [-cs for cs in css]
            continue
        ys = [cs + _dot(e.astype(BF16), cs.astype(BF16)) for e, cs in zip(es, css)]
        es = [e - y - _dot(y.astype(BF16), e.astype(BF16)) for e, y in zip(es, ys)]
    return es


def _gdn_intra(qs, ks, vs, gs, bs, us, wb, qb, ab, kd, items):
    c = GDN_CHUNK
    row = lax.broadcasted_iota(jnp.int32, (c, c), 0)
    col = lax.broadcasted_iota(jnp.int32, (c, c), 1)
    work = []
    for j, d, r0 in items:
        rows = pl.ds(r0, c)
        qc = qs[j, rows, :]
        kc = ks[j, rows, :]
        gcum = gs[j, d, rows, :]
        beta = bs[j, d, rows, :]
        incl = (row >= col) if d == 0 else (row <= col)
        dm = gcum - gcum.T
        decay = jnp.where(incl, jnp.exp(jnp.where(incl, dm, 0.0)), 0.0)
        kb = kc * beta
        work.append(dict(j=j, d=d, rows=rows, qc=qc, kc=kc, gcum=gcum, beta=beta, decay=decay, kb=kb))
    for w in work:
        w["kq"] = _dot_nt(jnp.concatenate([w["kb"], w["qc"]], axis=0).astype(BF16), w["kc"].astype(BF16))
    lms = []
    for w in work:
        strict = (row > col) if w["d"] == 0 else (row < col)
        lms.append(jnp.where(strict, w["kq"][:c] * w["decay"], 0.0))
    es = _unit_tri_inverses_minus_eye(lms)
    for w, e in zip(work, es):
        j, d, rows = w["j"], w["d"], w["rows"]
        egc = jnp.exp(w["gcum"])
        rhs = jnp.concatenate([vs[j, rows, :] * w["beta"], w["kb"] * egc], axis=1)
        sol = rhs + _dot(e.astype(BF16), rhs.astype(BF16))
        g_last = w["gcum"][c - 1:c, :] if d == 0 else w["gcum"][0:1, :]
        us[j, d, rows, :] = sol[:, :DK]
        wb[j, d, rows, :] = sol[:, DK:].astype(BF16)
        qb[j, d, rows, :] = (w["qc"] * egc).astype(BF16)
        ab[j, d, rows, :] = (w["kq"][c:] * w["decay"]).astype(BF16)
        kd[j, d, rows, :] = (w["kc"] * jnp.exp(g_last - w["gcum"])).astype(BF16)


def _gdn_scan_step(gs, us, wb, qb, ab, kd, ss, os, j, d, r0):
    c = GDN_CHUNK
    rows = pl.ds(r0, c)
    s_state = ss[j, d]
    ws_qs = _dot(jnp.concatenate([wb[j, d, rows, :], qb[j, d, rows, :]], axis=0), s_state.astype(BF16))
    v_new = (us[j, d, rows, :] - ws_qs[:c]).astype(BF16)
    os[j, rows, :] += ws_qs[c:] + _dot(ab[j, d, rows, :], v_new)
    g_last = gs[j, d, pl.ds(r0 + (c - 1 if d == 0 else 0), 1), :]
    ss[j, d] = s_state * jnp.exp(g_last) + _dot_tn(kd[j, d, rows, :], v_new)


def _gdn_kernel(q_ref, k_ref, v_ref, z_ref, ba_ref, arow_ref, drow_ref, cwq_ref, cwk_ref, cwv_ref, nw_ref, *rest,
                has_s0, has_sout):
    rest = list(rest)
    s0_ref = rest.pop(0) if has_s0 else None
    o_ref = rest.pop(0)
    sout_ref = rest.pop(0) if has_sout else None
    qs, ks, vs, gate, gs, bs, us, wb, qb, ab, kd, ss, os = rest
    hp = pl.program_id(1)
    t = q_ref.shape[0]
    lane = lax.broadcasted_iota(jnp.int32, (t, LANE), 1)

    @pl.when(hp == 0)
    def _():
        ba = ba_ref[...]
        x = ba + drow_ref[...]
        softplus = jnp.maximum(x, 0.0) + jnp.log(1.0 + jnp.exp(-jnp.abs(x)))
        g = -jnp.exp(arow_ref[...]) * softplus
        gcum = jnp.where(lane >= 3 * H_GDN, _seg_cumsum(g, True), _seg_cumsum(g, False))
        gate[...] = jnp.where(lane < 2 * H_GDN, _sigmoid(ba), gcum)

    for j in range(GDN_HB):
        h = hp * GDN_HB + j
        sl = slice(j * DK, (j + 1) * DK)
        q = _conv_silu(q_ref[:, sl], cwq_ref, sl)
        qs[j] = q * lax.rsqrt(jnp.sum(q * q, axis=-1, keepdims=True) + RMS_EPS) * (DK ** -0.5)
        k = _conv_silu(k_ref[:, sl], cwk_ref, sl)
        ks[j] = k * lax.rsqrt(jnp.sum(k * k, axis=-1, keepdims=True) + RMS_EPS)
        vs[j] = _conv_silu(v_ref[:, sl], cwv_ref, sl)
        gt = gate[...]
        for d in range(2):
            bcol = jnp.sum(jnp.where(lane == d * H_GDN + h, gt, 0.0), axis=-1, keepdims=True)
            gcol = jnp.sum(jnp.where(lane == (2 + d) * H_GDN + h, gt, 0.0), axis=-1, keepdims=True)
            bs[j, d] = jnp.broadcast_to(bcol, (t, LANE))
            gs[j, d] = jnp.broadcast_to(gcol, (t, LANE))
            ss[j, d] = s0_ref[d, j] if has_s0 else jnp.zeros((DK, DK), F32)
        os[j] = jnp.zeros((t, DK), F32)

    n_chunks = t // GDN_CHUNK
    n_groups = n_chunks // GDN_CG

    def intra_group(i):
        items = []
        for j in range(GDN_HB):
            for d in range(2):
                for cc in range(GDN_CG):
                    r0 = (i * GDN_CG + cc) * GDN_CHUNK
                    if not isinstance(r0, int):
                        r0 = pl.multiple_of(r0, GDN_CHUNK)
                    items.append((j, d, r0))
        _gdn_intra(qs, ks, vs, gs, bs, us, wb, qb, ab, kd, items)

    def scan_step(i):
        for j in range(GDN_HB):
            for d in range(2):
                r0 = (i if d == 0 else n_chunks - 1 - i) * GDN_CHUNK
                if not isinstance(r0, int):
                    r0 = pl.multiple_of(r0, GDN_CHUNK)
                _gdn_scan_step(gs, us, wb, qb, ab, kd, ss, os, j, d, r0)

    if n_groups == 1:
        intra_group(0)
    else:
        lax.fori_loop(0, n_groups, lambda i, c: (intra_group(i), c)[1], 0)
    if n_chunks <= 2:
        for i in range(n_chunks):
            scan_step(i)
    else:
        lax.fori_loop(0, n_chunks, lambda i, c: (scan_step(i), c)[1], 0)

    for j in range(GDN_HB):
        sl = slice(j * DK, (j + 1) * DK)
        o_ref[:, sl] = (_rms_norm(os[j], nw_ref[...]) * _silu(z_ref[:, sl])).astype(o_ref.dtype)
        if has_sout:
            for d in range(2):
                sout_ref[d, j] = ss[j, d]


def _gdn(proj, conv_w, a_row, d_row, norm_w, s0, l, bsz, t, want_state):
    has_s0 = s0 is not None
    hb = GDN_HB
    wblk = hb * DK
    col = lambda off: (lambda b, hp: (b, off // wblk + hp))
    row0 = lambda b, hp: (0, 0)
    in_specs = [
        pl.BlockSpec((t, wblk), col(0)),
        pl.BlockSpec((t, wblk), col(GDN_W)),
        pl.BlockSpec((t, wblk), col(2 * GDN_W)),
        pl.BlockSpec((t, wblk), col(COL_Z)),
        pl.BlockSpec((t, LANE), lambda b, hp: (b, COL_BA // LANE)),
        pl.BlockSpec((1, LANE), row0),
        pl.BlockSpec((1, LANE), row0),
        pl.BlockSpec((CONV_K, wblk), lambda b, hp: (0, hp)),
        pl.BlockSpec((CONV_K, wblk), lambda b, hp: (0, GDN_W // wblk + hp)),
        pl.BlockSpec((CONV_K, wblk), lambda b, hp: (0, 2 * GDN_W // wblk + hp)),
        pl.BlockSpec((1, DK), row0),
    ]
    args = [proj, proj, proj, proj, proj, a_row, d_row, conv_w, conv_w, conv_w, norm_w]
    if has_s0:
        in_specs.append(pl.BlockSpec((None, None, 2, hb, DK, DK), lambda b, hp: (b, l, 0, hp, 0, 0)))
        args.append(s0)
    out_specs = [pl.BlockSpec((t, wblk), lambda b, hp: (b, hp))]
    out_shape = [jax.ShapeDtypeStruct((bsz * t, GDN_W), BF16)]
    if want_state:
        out_specs.append(pl.BlockSpec((None, 2, hb, DK, DK), lambda b, hp: (b, 0, hp, 0, 0)))
        out_shape.append(jax.ShapeDtypeStruct((bsz, 2, H_GDN, DK, DK), F32))
    per_head = lambda dt: pltpu.VMEM((hb, t, DK), dt)
    per_dir = lambda dt: pltpu.VMEM((hb, 2, t, DK), dt)
    return pl.pallas_call(
        functools.partial(_gdn_kernel, has_s0=has_s0, has_sout=want_state),
        grid=(bsz, H_GDN // hb),
        in_specs=in_specs,
        out_specs=out_specs,
        out_shape=out_shape,
        scratch_shapes=[
            per_head(F32), per_head(F32), per_head(F32),
            pltpu.VMEM((t, LANE), F32),
            per_dir(F32), per_dir(F32),
            per_dir(F32), per_dir(BF16), per_dir(BF16), per_dir(BF16), per_dir(BF16),
            pltpu.VMEM((hb, 2, DK, DK), F32),
            per_head(F32),
        ],
        compiler_params=_cparams(("parallel", "arbitrary")),
        name="gdn_lat" if has_s0 else "gdn_ctx",
    )(*args)


def _rope(x, cos, sin):
    lane = lax.broadcasted_iota(jnp.int32, x.shape, 1)
    quarter = HEAD_DIM // 4
    partner = jnp.where((lane % (2 * quarter)) < quarter,
                        pltpu.roll(x, shift=HEAD_DIM - quarter, axis=1),
                        pltpu.roll(x, shift=quarter, axis=1))
    return x * cos + partner * sin


def _softmax_pv(score_blocks, value_blocks):
    m = None
    for s in score_blocks:
        bm = jnp.max(s, axis=-1, keepdims=True)
        m = bm if m is None else jnp.maximum(m, bm)
    den, acc = None, None
    for s, v in zip(score_blocks, value_blocks):
        p = jnp.exp(s - m)
        ps = jnp.sum(p, axis=-1, keepdims=True)
        pv = _dot(p.astype(BF16), v)
        den = ps if den is None else den + ps
        acc = pv if acc is None else acc + pv
    return acc / den


def _attn_ctx_kernel(q_ref, k_ref, v_ref, qw_ref, kw_ref, o_ref, kc_ref, vc_ref):
    scale = HEAD_DIM ** -0.5
    vc_ref[...] = v_ref[...]
    for g in range(N_KV):
        gl = slice(g * HEAD_DIM, (g + 1) * HEAD_DIM)
        kg = _rms_norm(k_ref[:, gl], kw_ref[...])
        kc_ref[:, gl] = kg
        kgb = kg.astype(BF16)
        vgb = v_ref[:, gl].astype(BF16)
        for hh in range(GRP):
            hl = slice((g * GRP + hh) * HEAD_DIM, (g * GRP + hh + 1) * HEAD_DIM)
            qh = _rms_norm(q_ref[:, hl], qw_ref[...]).astype(BF16)
            s = _dot_nt(qh, kgb) * scale
            o_ref[:, hl] = _softmax_pv([s], [vgb]).astype(o_ref.dtype)


def _attn_ctx(proj, q_norm_w, k_norm_w, bsz, t):
    return pl.pallas_call(
        _attn_ctx_kernel,
        grid=(bsz,),
        in_specs=[
            pl.BlockSpec((t, ATTN_W), lambda b: (b, COL_Q // ATTN_W)),
            pl.BlockSpec((t, KV_W), lambda b: (b, COL_K // KV_W)),
            pl.BlockSpec((t, KV_W), lambda b: (b, COL_V // KV_W)),
            pl.BlockSpec((1, HEAD_DIM), lambda b: (0, 0)),
            pl.BlockSpec((1, HEAD_DIM), lambda b: (0, 0)),
        ],
        out_specs=[
            pl.BlockSpec((t, ATTN_W), lambda b: (b, 0)),
            pl.BlockSpec((None, t, KV_W), lambda b: (b, 0, 0)),
            pl.BlockSpec((None, t, KV_W), lambda b: (b, 0, 0)),
        ],
        out_shape=[
            jax.ShapeDtypeStruct((bsz * t, ATTN_W), BF16),
            jax.ShapeDtypeStruct((bsz, t, KV_W), F32),
            jax.ShapeDtypeStruct((bsz, t, KV_W), F32),
        ],
        compiler_params=_cparams(("parallel",)),
        name="attn_ctx",
    )(proj, proj, proj, q_norm_w, k_norm_w)


def _attn_lat_kernel(q_ref, k_ref, v_ref, ck_ref, cv_ref, qw_ref, kw_ref, cq_ref, sq_ref, ck_cos_ref, ck_sin_ref,
                     o_ref):
    scale = HEAD_DIM ** -0.5
    for g in range(N_KV):
        gl = slice(g * HEAD_DIM, (g + 1) * HEAD_DIM)
        kg = _rope(_rms_norm(k_ref[:, gl], kw_ref[...]), ck_cos_ref[...], ck_sin_ref[...]).astype(BF16)
        vg = v_ref[:, gl].astype(BF16)
        ckg = ck_ref[:, gl].astype(BF16)
        cvg = cv_ref[:, gl].astype(BF16)
        for hh in range(GRP):
            hl = slice((g * GRP + hh) * HEAD_DIM, (g * GRP + hh + 1) * HEAD_DIM)
            qh = _rope(_rms_norm(q_ref[:, hl], qw_ref[...]), cq_ref[...], sq_ref[...]).astype(BF16)
            s1 = _dot_nt(qh, kg) * scale
            s2 = _dot_nt(qh, ckg) * scale
            o_ref[:, hl] = _softmax_pv([s1, s2], [vg, cvg]).astype(o_ref.dtype)


def _attn_lat(proj, cache_k, cache_v, q_norm_w, k_norm_w, cos_t, sin_t, l, bsz, t, tq):
    past = cache_k.shape[2]
    nq = t // tq
    return pl.pallas_call(
        _attn_lat_kernel,
        grid=(bsz, nq),
        in_specs=[
            pl.BlockSpec((tq, ATTN_W), lambda b, i: (b * nq + i, COL_Q // ATTN_W)),
            pl.BlockSpec((t, KV_W), lambda b, i: (b, COL_K // KV_W)),
            pl.BlockSpec((t, KV_W), lambda b, i: (b, COL_V // KV_W)),
            pl.BlockSpec((None, None, past, KV_W), lambda b, i: (b, l, 0, 0)),
            pl.BlockSpec((None, None, past, KV_W), lambda b, i: (b, l, 0, 0)),
            pl.BlockSpec((1, HEAD_DIM), lambda b, i: (0, 0)),
            pl.BlockSpec((1, HEAD_DIM), lambda b, i: (0, 0)),
            pl.BlockSpec((tq, HEAD_DIM), lambda b, i: (i, 0)),
            pl.BlockSpec((tq, HEAD_DIM), lambda b, i: (i, 0)),
            pl.BlockSpec((t, HEAD_DIM), lambda b, i: (0, 0)),
            pl.BlockSpec((t, HEAD_DIM), lambda b, i: (0, 0)),
        ],
        out_specs=pl.BlockSpec((tq, ATTN_W), lambda b, i: (b * nq + i, 0)),
        out_shape=jax.ShapeDtypeStruct((bsz * t, ATTN_W), BF16),
        compiler_params=_cparams(("parallel", "arbitrary")),
        name="attn_lat",
    )(proj, proj, proj, cache_k, cache_v, q_norm_w, k_norm_w, cos_t, sin_t, cos_t, sin_t)


def _rope_lane_tables(n_tok):
    rows = n_tok // GRID_W
    row = jnp.repeat(jnp.arange(rows), GRID_W).astype(F32)
    colp = jnp.tile(jnp.arange(GRID_W), rows).astype(F32)
    half = HEAD_DIM // 2
    inv_freq = ROPE_BASE ** (-jnp.arange(0, half, 2, dtype=F32) / half)
    ang_r = row[:, None] * inv_freq
    ang_c = colp[:, None] * inv_freq
    cos_t = jnp.concatenate([jnp.cos(ang_r), jnp.cos(ang_r), jnp.cos(ang_c), jnp.cos(ang_c)], axis=-1)
    sin_t = jnp.concatenate([-jnp.sin(ang_r), jnp.sin(ang_r), -jnp.sin(ang_c), jnp.sin(ang_c)], axis=-1)
    return cos_t, sin_t


def _outproj_kernel(og_ref, oa_ref, wg_ref, wa_ref, x_ref, m_ref, g_ref, b_ref, o_ref):
    mix = _dot(og_ref[...], wg_ref[...]) + _dot(oa_ref[...], wa_ref[...])
    y = DN_ALPHA * x_ref[...] + m_ref[2:3, :] * mix
    o_ref[...] = _layer_norm(y, g_ref[...], b_ref[...])


def _out_proj(o_gdn, o_att, w_o, x, mod, ln_g, ln_b, tiles_per_group):
    n = x.shape[0]
    row = lambda i: (i, 0)
    const = lambda i: (0, 0)
    return pl.pallas_call(
        _outproj_kernel,
        grid=(n // TM,),
        in_specs=[
            pl.BlockSpec((TM, GDN_W), row),
            pl.BlockSpec((TM, ATTN_W), row),
            pl.BlockSpec((GDN_W, D_MODEL), lambda i: (0, 0)),
            pl.BlockSpec((ATTN_W, D_MODEL), lambda i: (1, 0)),
            pl.BlockSpec((TM, D_MODEL), row),
            pl.BlockSpec((None, 6, D_MODEL), lambda i: (i // tiles_per_group, 0, 0)),
            pl.BlockSpec((1, D_MODEL), const),
            pl.BlockSpec((1, D_MODEL), const),
        ],
        out_specs=pl.BlockSpec((TM, D_MODEL), row),
        out_shape=jax.ShapeDtypeStruct((n, D_MODEL), F32),
        compiler_params=_cparams(("parallel",)),
        name="out_proj",
    )(o_gdn, o_att, w_o, w_o, x, mod, ln_g, ln_b)


def _ffn_up_kernel(x_ref, m_ref, wg_ref, wu_ref, o_ref, h_ref):
    @pl.when(pl.program_id(1) == 0)
    def _():
        h_ref[...] = (x_ref[...] * (1.0 + m_ref[4:5, :]) + m_ref[3:4, :]).astype(BF16)

    h = h_ref[...]
    gate = _dot(h, wg_ref[...])
    up = _dot(h, wu_ref[...])
    o_ref[...] = (_silu(gate) * up).astype(o_ref.dtype)


def _ffn_up(x, mod, w_gu, tiles_per_group):
    n = x.shape[0]
    nj = D_FF // TN_FF
    return pl.pallas_call(
        _ffn_up_kernel,
        grid=(n // TM, nj),
        in_specs=[
            pl.BlockSpec((TM, D_MODEL), lambda i, j: (i, 0)),
            pl.BlockSpec((None, 6, D_MODEL), lambda i, j: (i // tiles_per_group, 0, 0)),
            pl.BlockSpec((D_MODEL, TN_FF), lambda i, j: (0, j)),
            pl.BlockSpec((D_MODEL, TN_FF), lambda i, j: (0, nj + j)),
        ],
        out_specs=pl.BlockSpec((TM, TN_FF), lambda i, j: (i, j)),
        out_shape=jax.ShapeDtypeStruct((n, D_FF), BF16),
        scratch_shapes=[pltpu.VMEM((TM, D_MODEL), BF16)],
        compiler_params=_cparams(("parallel", "arbitrary")),
        name="ffn_up",
    )(x, mod, w_gu, w_gu)


def _ffn_down_kernel(a_ref, w_ref, x_ref, m_ref, g_ref, b_ref, o_ref, acc_ref):
    k = pl.program_id(1)

    @pl.when(k == 0)
    def _():
        acc_ref[...] = jnp.zeros_like(acc_ref)

    acc_ref[...] += _dot(a_ref[...], w_ref[...])

    @pl.when(k == pl.num_programs(1) - 1)
    def _():
        y = DN_ALPHA * x_ref[...] + m_ref[5:6, :] * acc_ref[...]
        o_ref[...] = _layer_norm(y, g_ref[...], b_ref[...])


def _ffn_down(act, w_down, x, mod, ln_g, ln_b, tiles_per_group):
    n = x.shape[0]
    return pl.pallas_call(
        _ffn_down_kernel,
        grid=(n // TM, D_FF // TK_FF),
        in_specs=[
            pl.BlockSpec((TM, TK_FF), lambda i, k: (i, k)),
            pl.BlockSpec((TK_FF, D_MODEL), lambda i, k: (k, 0)),
            pl.BlockSpec((TM, D_MODEL), lambda i, k: (i, 0)),
            pl.BlockSpec((None, 6, D_MODEL), lambda i, k: (i // tiles_per_group, 0, 0)),
            pl.BlockSpec((1, D_MODEL), lambda i, k: (0, 0)),
            pl.BlockSpec((1, D_MODEL), lambda i, k: (0, 0)),
        ],
        out_specs=pl.BlockSpec((TM, D_MODEL), lambda i, k: (i, 0)),
        out_shape=jax.ShapeDtypeStruct((n, D_MODEL), F32),
        scratch_shapes=[pltpu.VMEM((TM, D_MODEL), F32)],
        compiler_params=_cparams(("parallel", "arbitrary")),
        name="ffn_down",
    )(act, w_down, x, mod, ln_g, ln_b)


def _trunk_layer(x, mod, lw, l, bsz, t, latent):
    assert t % TM == 0 or TM % t == 0
    tiles_per_group = max(t // TM, 1) if latent is not None else x.shape[0] // TM
    proj = _in_proj(x, mod, lw["w_in"], tiles_per_group)
    if latent is None:
        o_gdn, state = _gdn(proj, lw["conv_w"], lw["a_row"], lw["d_row"], lw["gdn_norm_w"], None, l, bsz, t, True)
        o_att, k_c, v_c = _attn_ctx(proj, lw["q_norm_w"], lw["k_norm_w"], bsz, t)
    else:
        cache_k, cache_v, state_gdn, cos_t, sin_t = latent
        (o_gdn,) = _gdn(proj, lw["conv_w"], lw["a_row"], lw["d_row"], lw["gdn_norm_w"], state_gdn, l, bsz, t, False)
        o_att = _attn_lat(proj, cache_k, cache_v, lw["q_norm_w"], lw["k_norm_w"], cos_t, sin_t, l, bsz, t, 256)
        state = k_c = v_c = None
    x1 = _out_proj(o_gdn, o_att, lw["w_o"], x, mod, lw["ln1_g"], lw["ln1_b"], tiles_per_group)
    act = _ffn_up(x1, mod, lw["w_gate_up"], tiles_per_group)
    x2 = _ffn_down(act, lw["w_down"], x1, mod, lw["ln2_g"], lw["ln2_b"], tiles_per_group)
    return x2, k_c, v_c, state


def kernel(x_prompt, x_sample, cache_k, cache_v, state_gdn, c, c_ctx, w_ada, b_ada, w_in, conv_w, a_log, dt_bias,
           gdn_norm_w, q_norm_w, k_norm_w, w_o, ln1_g, ln1_b, ln2_g, ln2_b, w_gate_up, w_down):
    bsz, seq, _ = x_prompt.shape
    dbsz, dseq, _ = x_sample.shape
    past = cache_k.shape[2]

    i_b = QKV_W + GDN_W
    i_q = i_b + 4 * H_GDN
    w_in_r = jnp.concatenate(
        [w_in[:, :, :i_b], w_in[:, :, i_q:], w_in[:, :, i_b:i_q],
         jnp.zeros((DEPTH, D_MODEL, LANE - 4 * H_GDN), w_in.dtype)], axis=-1).astype(BF16)
    w_o_b = w_o.astype(BF16)
    w_gu_b = w_gate_up.astype(BF16)
    w_down_b = w_down.astype(BF16)

    cond8 = jnp.concatenate([c_ctx[None, :], c, jnp.zeros((8 - 1 - dbsz, D_MODEL), F32)], axis=0)
    mod = _ada(cond8, w_ada, b_ada.reshape(DEPTH, 1, 6 * D_MODEL)).reshape(DEPTH, 8, 6, D_MODEL)

    cos_t, sin_t = _rope_lane_tables(dseq)
    ck = cache_k.reshape(dbsz, DEPTH, past, KV_W)
    cv = cache_v.reshape(dbsz, DEPTH, past, KV_W)

    def gate_rows(p):
        return jnp.zeros((1, LANE), F32).at[0, 2 * H_GDN:4 * H_GDN].set(p.reshape(-1))

    xp = x_prompt.reshape(bsz * seq, D_MODEL)
    xs = x_sample.reshape(dbsz * dseq, D_MODEL)
    ks_out, vs_out, ss_out = [], [], []
    for l in range(DEPTH):
        lw = {
            "w_in": w_in_r[l], "conv_w": conv_w[l], "a_row": gate_rows(a_log[l]), "d_row": gate_rows(dt_bias[l]),
            "gdn_norm_w": gdn_norm_w[l].reshape(1, DK), "q_norm_w": q_norm_w[l].reshape(1, HEAD_DIM),
            "k_norm_w": k_norm_w[l].reshape(1, HEAD_DIM), "w_o": w_o_b[l],
            "ln1_g": ln1_g[l].reshape(1, D_MODEL), "ln1_b": ln1_b[l].reshape(1, D_MODEL),
            "ln2_g": ln2_g[l].reshape(1, D_MODEL), "ln2_b": ln2_b[l].reshape(1, D_MODEL),
            "w_gate_up": w_gu_b[l], "w_down": w_down_b[l],
        }
        xp, k_c, v_c, s_c = _trunk_layer(xp, mod[l, 0:1], lw, l, bsz, seq, None)
        ks_out.append(k_c.reshape(bsz, seq, N_KV, HEAD_DIM))
        vs_out.append(v_c.reshape(bsz, seq, N_KV, HEAD_DIM))
        ss_out.append(s_c)
        xs, _, _, _ = _trunk_layer(xs, mod[l, 1:1 + dbsz], lw, l, dbsz, dseq, (ck, cv, state_gdn, cos_t, sin_t))
    return (xp.reshape(bsz, seq, D_MODEL), xs.reshape(dbsz, dseq, D_MODEL),
            jnp.stack(ks_out, axis=1), jnp.stack(vs_out, axis=1), jnp.stack(ss_out, axis=1))
```

```python
import functools
import math

import jax
import jax.numpy as jnp
from jax import lax
from jax.experimental import pallas as pl
from jax.experimental.pallas import tpu as pltpu

D_MODEL = 2048
DEPTH = 2
GRID_W = 64
DK = 128
H_GDN = 8
GDN_W = H_GDN * DK
HEAD_DIM = 128
N_Q = 8
N_KV = 2
GRP = N_Q // N_KV
ATTN_W = N_Q * HEAD_DIM
KV_W = N_KV * HEAD_DIM
CONV_K = 5
D_FF = 5632
ROPE_BASE = 10000.0
RMS_EPS = 1e-6
LN_EPS = 1e-5
DN_ALPHA = (2 * DEPTH) ** 0.25
QKV_W = 3 * GDN_W
IN_W = QKV_W + GDN_W + 4 * H_GDN + ATTN_W + 2 * KV_W

LANE = 128
GDN_CHUNK = 128
GDN_HB = 2
GDN_HB_SHORT = 4
GDN_SHORT_T = 256
GDN_CG = 2
COL_Z = QKV_W
COL_Q = COL_Z + GDN_W
COL_K = COL_Q + ATTN_W
COL_V = COL_K + KV_W
COL_BA = COL_V + KV_W
IN_W_PAD = COL_BA + LANE

TM = 512
TM_UP = 1024
TN_IN = 1920
TN_FF = 512
TK_FF = 2816
TN_ADA = 1024
VMEM_LIMIT = 56 * 1024 * 1024

F32 = jnp.float32
BF16 = jnp.bfloat16


def _cparams(sem):
    return pltpu.CompilerParams(dimension_semantics=sem, vmem_limit_bytes=VMEM_LIMIT)


def _dot(a, b):
    return jnp.dot(a, b, preferred_element_type=F32)


def _dot_nt(a, b):
    return lax.dot_general(a, b, (((1,), (1,)), ((), ())), preferred_element_type=F32)


def _dot_tn(a, b):
    return lax.dot_general(a, b, (((0,), (0,)), ((), ())), preferred_element_type=F32)


def _sigmoid(x):
    return 1.0 / (1.0 + jnp.exp(-x))


def _silu(x):
    return x * _sigmoid(x)


def _layer_norm(y, g, b):
    mu = jnp.mean(y, axis=-1, keepdims=True)
    yc = y - mu
    var = jnp.mean(yc * yc, axis=-1, keepdims=True)
    return yc * lax.rsqrt(var + LN_EPS) * g + b


def _rms_norm(x, w):
    return x * lax.rsqrt(jnp.mean(x * x, axis=-1, keepdims=True) + RMS_EPS) * w


def _ada_kernel(c_ref, w_ref, b_ref, o_ref):
    c = c_ref[...]
    o_ref[...] = _dot(_silu(c).astype(BF16), w_ref[...].astype(BF16)) + b_ref[...]


def _ada(cond8, w_ada, b_ada):
    n = w_ada.shape[-1]
    return pl.pallas_call(
        _ada_kernel,
        grid=(DEPTH, n // TN_ADA),
        in_specs=[
            pl.BlockSpec((8, D_MODEL), lambda l, j: (0, 0)),
            pl.BlockSpec((None, D_MODEL, TN_ADA), lambda l, j: (l, 0, j)),
            pl.BlockSpec((None, 1, TN_ADA), lambda l, j: (l, 0, j)),
        ],
        out_specs=pl.BlockSpec((None, 8, TN_ADA), lambda l, j: (l, 0, j)),
        out_shape=jax.ShapeDtypeStruct((DEPTH, 8, n), F32),
        compiler_params=_cparams(("arbitrary", "arbitrary")),
        name="ada",
    )(cond8, w_ada, b_ada)


def _inproj_kernel(x_ref, m_ref, w_ref, o_ref, h_ref):
    @pl.when(pl.program_id(1) == 0)
    def _():
        h_ref[...] = (x_ref[...] * (1.0 + m_ref[1:2, :]) + m_ref[0:1, :]).astype(BF16)

    o_ref[...] = _dot(h_ref[...], w_ref[...])


def _in_proj(x, mod, w, l, tiles_per_group):
    n = x.shape[0]
    return pl.pallas_call(
        _inproj_kernel,
        grid=(n // TM, IN_W_PAD // TN_IN),
        in_specs=[
            pl.BlockSpec((TM, D_MODEL), lambda i, j: (i, 0)),
            pl.BlockSpec((None, 6, D_MODEL), lambda i, j: (i // tiles_per_group, 0, 0)),
            pl.BlockSpec((None, D_MODEL, TN_IN), lambda i, j: (l, 0, j)),
        ],
        out_specs=pl.BlockSpec((TM, TN_IN), lambda i, j: (i, j)),
        out_shape=jax.ShapeDtypeStruct((n, IN_W_PAD), F32),
        scratch_shapes=[pltpu.VMEM((TM, D_MODEL), BF16)],
        compiler_params=_cparams(("parallel", "arbitrary")),
        name="in_proj",
    )(x, mod, w)


def _conv_silu(x, w_ref, cols):
    t = x.shape[0]
    row = lax.broadcasted_iota(jnp.int32, x.shape, 0)
    acc = None
    for j in range(CONV_K):
        s = CONV_K // 2 - j
        if s == 0:
            xs = x
        else:
            xs = pltpu.roll(x, shift=s % t, axis=0)
            valid = (row >= s) if s > 0 else (row < t + s)
            xs = jnp.where(valid, xs, 0.0)
        term = xs * w_ref[j:j + 1, cols]
        acc = term if acc is None else acc + term
    return _silu(acc)


def _seg_cumsum(x, reverse):
    t = x.shape[0]
    pos = lax.broadcasted_iota(jnp.int32, x.shape, 0) % GDN_CHUNK
    s = 1
    while s < GDN_CHUNK:
        if reverse:
            x = x + jnp.where(pos < GDN_CHUNK - s, pltpu.roll(x, shift=t - s, axis=0), 0.0)
        else:
            x = x + jnp.where(pos >= s, pltpu.roll(x, shift=s, axis=0), 0.0)
        s *= 2
    return x


def _unit_tri_inverses(lms):
    c = lms[0].shape[0]
    row = lax.broadcasted_iota(jnp.int32, (c, c), 0)
    col = lax.broadcasted_iota(jnp.int32, (c, c), 1)
    rc = row ^ col
    eye = (row == col).astype(F32)
    xs = None
    for lvl in range(int(math.log2(c))):
        keep = (rc >> lvl) == 1
        if xs is None:
            xs = [eye - jnp.where(keep, lm, 0.0) for lm in lms]
            lbs = [lm.astype(BF16) for lm in lms]
            zero = jnp.zeros((c, c), BF16)
            continue
        xbs = [x.astype(BF16) for x in xs]
        ys = [_dot(xb, jnp.where(keep, lb, zero)).astype(BF16) for xb, lb in zip(xbs, lbs)]
        xs = [x - _dot(y, xb) for x, y, xb in zip(xs, ys, xbs)]
    return xs


def _gdn_intra(qs, ks, vs, gs, bs, us, wb, qb, ab, kd, chunks):
    c = GDN_CHUNK
    row = lax.broadcasted_iota(jnp.int32, (c, c), 0)
    col = lax.broadcasted_iota(jnp.int32, (c, c), 1)
    incl = (row >= col, row <= col)
    strict = (row > col, row < col)
    kqs = []
    for j, r0 in chunks:
        rows = pl.ds(r0, c)
        kcb = ks[j, rows, :].astype(BF16)
        kqs.append(_dot_nt(jnp.concatenate([kcb, qs[j, rows, :].astype(BF16)], axis=0), kcb))
    work, lms = [], []
    for (j, r0), kq in zip(chunks, kqs):
        rows = pl.ds(r0, c)
        for d in range(2):
            gcum = gs[j, d, rows, :]
            beta = bs[j, d, rows, :]
            dm = gcum - gcum.T
            decay = jnp.where(incl[d], jnp.exp(jnp.where(incl[d], dm, 0.0)), 0.0)
            lms.append(jnp.where(strict[d], kq[:c] * beta * decay, 0.0))
            ab[j, d, rows, :] = (kq[c:] * decay).astype(BF16)
            work.append((j, d, rows, gcum, beta))
    xs = _unit_tri_inverses(lms)
    for (j, d, rows, gcum, beta), x in zip(work, xs):
        qc = qs[j, rows, :]
        kc = ks[j, rows, :]
        egc = jnp.exp(gcum)
        rhs = jnp.concatenate([vs[j, rows, :] * beta, kc * (beta * egc)], axis=1)
        sol = _dot(x.astype(BF16), rhs.astype(BF16))
        g_last = gcum[c - 1:c, :] if d == 0 else gcum[0:1, :]
        us[j, d, rows, :] = sol[:, :DK]
        wb[j, d, rows, :] = sol[:, DK:].astype(BF16)
        qb[j, d, rows, :] = (qc * egc).astype(BF16)
        kd[j, d, rows, :] = (kc * jnp.exp(g_last - gcum)).astype(BF16)


def _gdn_scan_step(gs, us, wb, qb, ab, kd, ss, os, j, d, r0):
    c = GDN_CHUNK
    rows = pl.ds(r0, c)
    s_state = ss[j, d]
    ws_qs = _dot(jnp.concatenate([wb[j, d, rows, :], qb[j, d, rows, :]], axis=0), s_state.astype(BF16))
    v_new = (us[j, d, rows, :] - ws_qs[:c]).astype(BF16)
    os[j, rows, :] += ws_qs[c:] + _dot(ab[j, d, rows, :], v_new)
    g_last = gs[j, d, pl.ds(r0 + (c - 1 if d == 0 else 0), 1), :]
    ss[j, d] = s_state * jnp.exp(g_last) + _dot_tn(kd[j, d, rows, :], v_new)


def _gdn_kernel(q_ref, k_ref, v_ref, z_ref, ba_ref, arow_ref, drow_ref, cwq_ref, cwk_ref, cwv_ref, nw_ref, *rest,
                has_s0, has_sout):
    rest = list(rest)
    s0_ref = rest.pop(0) if has_s0 else None
    if has_sout:
        rest.pop(0)
    o_ref = rest.pop(0)
    sout_ref = rest.pop(0) if has_sout else None
    qs, ks, vs, gate, gs, bs, us, wb, qb, ab, kd, ss, os = rest
    hp = pl.program_id(1)
    t = q_ref.shape[0]
    hb = q_ref.shape[1] // DK
    lane = lax.broadcasted_iota(jnp.int32, (t, LANE), 1)

    @pl.when(hp == 0)
    def _():
        ba = ba_ref[...]
        x = ba + drow_ref[...]
        softplus = jnp.maximum(x, 0.0) + jnp.log(1.0 + jnp.exp(-jnp.abs(x)))
        g = -jnp.exp(arow_ref[...]) * softplus
        gcum = jnp.where(lane >= 3 * H_GDN, _seg_cumsum(g, True), _seg_cumsum(g, False))
        gate[...] = jnp.where(lane < 2 * H_GDN, _sigmoid(ba), gcum)

    for j in range(hb):
        h = hp * hb + j
        sl = slice(j * DK, (j + 1) * DK)
        q = _conv_silu(q_ref[:, sl], cwq_ref, sl)
        qs[j] = q * lax.rsqrt(jnp.sum(q * q, axis=-1, keepdims=True) + RMS_EPS) * (DK ** -0.5)
        k = _conv_silu(k_ref[:, sl], cwk_ref, sl)
        ks[j] = k * lax.rsqrt(jnp.sum(k * k, axis=-1, keepdims=True) + RMS_EPS)
        vs[j] = _conv_silu(v_ref[:, sl], cwv_ref, sl)
        gt = gate[...]
        for d in range(2):
            bcol = jnp.sum(jnp.where(lane == d * H_GDN + h, gt, 0.0), axis=-1, keepdims=True)
            gcol = jnp.sum(jnp.where(lane == (2 + d) * H_GDN + h, gt, 0.0), axis=-1, keepdims=True)
            bs[j, d] = jnp.broadcast_to(bcol, (t, LANE))
            gs[j, d] = jnp.broadcast_to(gcol, (t, LANE))
            ss[j, d] = s0_ref[d, j] if has_s0 else jnp.zeros((DK, DK), F32)
        os[j] = jnp.zeros((t, DK), F32)

    n_chunks = t // GDN_CHUNK
    n_groups = n_chunks // GDN_CG

    def intra_group(i):
        chunks = []
        for j in range(hb):
            for cc in range(GDN_CG):
                r0 = (i * GDN_CG + cc) * GDN_CHUNK
                if not isinstance(r0, int):
                    r0 = pl.multiple_of(r0, GDN_CHUNK)
                chunks.append((j, r0))
        _gdn_intra(qs, ks, vs, gs, bs, us, wb, qb, ab, kd, chunks)

    def scan_step(i):
        for j in range(hb):
            for d in range(2):
                r0 = (i if d == 0 else n_chunks - 1 - i) * GDN_CHUNK
                if not isinstance(r0, int):
                    r0 = pl.multiple_of(r0, GDN_CHUNK)
                _gdn_scan_step(gs, us, wb, qb, ab, kd, ss, os, j, d, r0)

    if n_groups == 1:
        intra_group(0)
    else:
        lax.fori_loop(0, n_groups, lambda i, c: (intra_group(i), c)[1], 0)
    if n_chunks <= 2:
        for i in range(n_chunks):
            scan_step(i)
    else:
        lax.fori_loop(0, n_chunks, lambda i, c: (scan_step(i), c)[1], 0)

    for j in range(hb):
        sl = slice(j * DK, (j + 1) * DK)
        o_ref[:, sl] = (_rms_norm(os[j], nw_ref[...]) * _silu(z_ref[:, sl])).astype(o_ref.dtype)
        if has_sout:
            for d in range(2):
                sout_ref[d, j] = ss[j, d]


def _gdn(proj, conv_w, a_row, d_row, norm_w, s0, s_all, l, bsz, t):
    has_s0 = s0 is not None
    want_state = s_all is not None
    hb = GDN_HB_SHORT if t <= GDN_SHORT_T else GDN_HB
    wblk = hb * DK
    col = lambda off: (lambda b, hp: (b, off // wblk + hp))
    row0 = lambda b, hp: (0, 0)
    in_specs = [
        pl.BlockSpec((t, wblk), col(0)),
        pl.BlockSpec((t, wblk), col(GDN_W)),
        pl.BlockSpec((t, wblk), col(2 * GDN_W)),
        pl.BlockSpec((t, wblk), col(COL_Z)),
        pl.BlockSpec((t, LANE), lambda b, hp: (b, COL_BA // LANE)),
        pl.BlockSpec((1, LANE), row0),
        pl.BlockSpec((1, LANE), row0),
        pl.BlockSpec((CONV_K, wblk), lambda b, hp: (0, hp)),
        pl.BlockSpec((CONV_K, wblk), lambda b, hp: (0, GDN_W // wblk + hp)),
        pl.BlockSpec((CONV_K, wblk), lambda b, hp: (0, 2 * GDN_W // wblk + hp)),
        pl.BlockSpec((1, DK), row0),
    ]
    args = [proj, proj, proj, proj, proj, a_row, d_row, conv_w, conv_w, conv_w, norm_w]
    if has_s0:
        in_specs.append(pl.BlockSpec((None, None, 2, hb, DK, DK), lambda b, hp: (b, l, 0, hp, 0, 0)))
        args.append(s0)
    out_specs = [pl.BlockSpec((t, wblk), lambda b, hp: (b, hp))]
    out_shape = [jax.ShapeDtypeStruct((bsz * t, GDN_W), BF16)]
    aliases = {}
    if want_state:
        aliases = {len(args): 1}
        in_specs.append(pl.BlockSpec(memory_space=pl.ANY))
        args.append(s_all)
        out_specs.append(pl.BlockSpec((None, None, 2, hb, DK, DK), lambda b, hp: (b, l, 0, hp, 0, 0)))
        out_shape.append(jax.ShapeDtypeStruct(s_all.shape, F32))
    per_head = lambda dt: pltpu.VMEM((hb, t, DK), dt)
    per_dir = lambda dt: pltpu.VMEM((hb, 2, t, DK), dt)
    return pl.pallas_call(
        functools.partial(_gdn_kernel, has_s0=has_s0, has_sout=want_state),
        grid=(bsz, H_GDN // hb),
        in_specs=in_specs,
        out_specs=out_specs,
        out_shape=out_shape,
        input_output_aliases=aliases,
        scratch_shapes=[
            per_head(F32), per_head(F32), per_head(F32),
            pltpu.VMEM((t, LANE), F32),
            per_dir(F32), per_dir(F32),
            per_dir(F32), per_dir(BF16), per_dir(BF16), per_dir(BF16), per_dir(BF16),
            pltpu.VMEM((hb, 2, DK, DK), F32),
            per_head(F32),
        ],
        compiler_params=_cparams(("parallel", "arbitrary")),
        name="gdn_lat" if has_s0 else "gdn_ctx",
    )(*args)


def _rope(x, cos, sin):
    lane = lax.broadcasted_iota(jnp.int32, x.shape, 1)
    quarter = HEAD_DIM // 4
    partner = jnp.where((lane % (2 * quarter)) < quarter,
                        pltpu.roll(x, shift=HEAD_DIM - quarter, axis=1),
                        pltpu.roll(x, shift=quarter, axis=1))
    return x * cos + partner * sin


def _softmax_pv(score_blocks, value_blocks):
    m = None
    for s in score_blocks:
        bm = jnp.max(s, axis=-1, keepdims=True)
        m = bm if m is None else jnp.maximum(m, bm)
    den, acc = None, None
    for s, v in zip(score_blocks, value_blocks):
        p = jnp.exp(s - m)
        ps = jnp.sum(p, axis=-1, keepdims=True)
        pv = _dot(p.astype(BF16), v)
        den = ps if den is None else den + ps
        acc = pv if acc is None else acc + pv
    return acc / den


def _attn_ctx_kernel(q_ref, k_ref, v_ref, qw_ref, kw_ref, kc_all_ref, vc_all_ref, o_ref, kc_ref, vc_ref):
    del kc_all_ref, vc_all_ref
    scale = HEAD_DIM ** -0.5
    vc_ref[...] = v_ref[...]
    for g in range(N_KV):
        gl = slice(g * HEAD_DIM, (g + 1) * HEAD_DIM)
        kg = _rms_norm(k_ref[:, gl], kw_ref[...])
        kc_ref[:, gl] = kg
        kgb = kg.astype(BF16)
        vgb = v_ref[:, gl].astype(BF16)
        for hh in range(GRP):
            hl = slice((g * GRP + hh) * HEAD_DIM, (g * GRP + hh + 1) * HEAD_DIM)
            qh = _rms_norm(q_ref[:, hl], qw_ref[...]).astype(BF16)
            s = _dot_nt(qh, kgb) * scale
            o_ref[:, hl] = _softmax_pv([s], [vgb]).astype(o_ref.dtype)


def _attn_ctx(proj, q_norm_w, k_norm_w, kc_all, vc_all, l, bsz, t):
    cache_spec = pl.BlockSpec((None, None, t, KV_W), lambda b: (b, l, 0, 0))
    return pl.pallas_call(
        _attn_ctx_kernel,
        grid=(bsz,),
        in_specs=[
            pl.BlockSpec((t, ATTN_W), lambda b: (b, COL_Q // ATTN_W)),
            pl.BlockSpec((t, KV_W), lambda b: (b, COL_K // KV_W)),
            pl.BlockSpec((t, KV_W), lambda b: (b, COL_V // KV_W)),
            pl.BlockSpec((1, HEAD_DIM), lambda b: (0, 0)),
            pl.BlockSpec((1, HEAD_DIM), lambda b: (0, 0)),
            pl.BlockSpec(memory_space=pl.ANY),
            pl.BlockSpec(memory_space=pl.ANY),
        ],
        out_specs=[pl.BlockSpec((t, ATTN_W), lambda b: (b, 0)), cache_spec, cache_spec],
        out_shape=[
            jax.ShapeDtypeStruct((bsz * t, ATTN_W), BF16),
            jax.ShapeDtypeStruct(kc_all.shape, F32),
            jax.ShapeDtypeStruct(vc_all.shape, F32),
        ],
        input_output_aliases={5: 1, 6: 2},
        compiler_params=_cparams(("parallel",)),
        name="attn_ctx",
    )(proj, proj, proj, q_norm_w, k_norm_w, kc_all, vc_all)


def _attn_lat_kernel(q_ref, k_ref, v_ref, ck_ref, cv_ref, qw_ref, kw_ref, cq_ref, sq_ref, ck_cos_ref, ck_sin_ref,
                     o_ref):
    scale = HEAD_DIM ** -0.5
    for g in range(N_KV):
        gl = slice(g * HEAD_DIM, (g + 1) * HEAD_DIM)
        kg = _rope(_rms_norm(k_ref[:, gl], kw_ref[...]), ck_cos_ref[...], ck_sin_ref[...]).astype(BF16)
        vg = v_ref[:, gl].astype(BF16)
        ckg = ck_ref[:, gl].astype(BF16)
        cvg = cv_ref[:, gl].astype(BF16)
        for hh in range(GRP):
            hl = slice((g * GRP + hh) * HEAD_DIM, (g * GRP + hh + 1) * HEAD_DIM)
            qh = _rope(_rms_norm(q_ref[:, hl], qw_ref[...]), cq_ref[...], sq_ref[...]).astype(BF16)
            s1 = _dot_nt(qh, kg) * scale
            s2 = _dot_nt(qh, ckg) * scale
            o_ref[:, hl] = _softmax_pv([s1, s2], [vg, cvg]).astype(o_ref.dtype)


def _attn_lat(proj, cache_k, cache_v, q_norm_w, k_norm_w, cos_t, sin_t, l, bsz, t, tq):
    past = cache_k.shape[2]
    nq = t // tq
    return pl.pallas_call(
        _attn_lat_kernel,
        grid=(bsz, nq),
        in_specs=[
            pl.BlockSpec((tq, ATTN_W), lambda b, i: (b * nq + i, COL_Q // ATTN_W)),
            pl.BlockSpec((t, KV_W), lambda b, i: (b, COL_K // KV_W)),
            pl.BlockSpec((t, KV_W), lambda b, i: (b, COL_V // KV_W)),
            pl.BlockSpec((None, None, past, KV_W), lambda b, i: (b, l, 0, 0)),
            pl.BlockSpec((None, None, past, KV_W), lambda b, i: (b, l, 0, 0)),
            pl.BlockSpec((1, HEAD_DIM), lambda b, i: (0, 0)),
            pl.BlockSpec((1, HEAD_DIM), lambda b, i: (0, 0)),
            pl.BlockSpec((tq, HEAD_DIM), lambda b, i: (i, 0)),
            pl.BlockSpec((tq, HEAD_DIM), lambda b, i: (i, 0)),
            pl.BlockSpec((t, HEAD_DIM), lambda b, i: (0, 0)),
            pl.BlockSpec((t, HEAD_DIM), lambda b, i: (0, 0)),
        ],
        out_specs=pl.BlockSpec((tq, ATTN_W), lambda b, i: (b * nq + i, 0)),
        out_shape=jax.ShapeDtypeStruct((bsz * t, ATTN_W), BF16),
        compiler_params=_cparams(("parallel", "arbitrary")),
        name="attn_lat",
    )(proj, proj, proj, cache_k, cache_v, q_norm_w, k_norm_w, cos_t, sin_t, cos_t, sin_t)


def _rope_lane_tables(n_tok):
    rows = n_tok // GRID_W
    row = jnp.repeat(jnp.arange(rows), GRID_W).astype(F32)
    colp = jnp.tile(jnp.arange(GRID_W), rows).astype(F32)
    half = HEAD_DIM // 2
    inv_freq = ROPE_BASE ** (-jnp.arange(0, half, 2, dtype=F32) / half)
    ang_r = row[:, None] * inv_freq
    ang_c = colp[:, None] * inv_freq
    cos_t = jnp.concatenate([jnp.cos(ang_r), jnp.cos(ang_r), jnp.cos(ang_c), jnp.cos(ang_c)], axis=-1)
    sin_t = jnp.concatenate([-jnp.sin(ang_r), jnp.sin(ang_r), -jnp.sin(ang_c), jnp.sin(ang_c)], axis=-1)
    return cos_t, sin_t


def _outproj_kernel(og_ref, oa_ref, wg_ref, wa_ref, x_ref, m_ref, g_ref, b_ref, o_ref):
    mix = _dot(og_ref[...], wg_ref[...]) + _dot(oa_ref[...], wa_ref[...])
    y = DN_ALPHA * x_ref[...] + m_ref[2:3, :] * mix
    o_ref[...] = _layer_norm(y, g_ref[...], b_ref[...])


def _out_proj(o_gdn, o_att, w_o, l, x, mod, ln_g, ln_b, tiles_per_group):
    n = x.shape[0]
    row = lambda i: (i, 0)
    const = lambda i: (0, 0)
    return pl.pallas_call(
        _outproj_kernel,
        grid=(n // TM,),
        in_specs=[
            pl.BlockSpec((TM, GDN_W), row),
            pl.BlockSpec((TM, ATTN_W), row),
            pl.BlockSpec((None, GDN_W, D_MODEL), lambda i: (l, 0, 0)),
            pl.BlockSpec((None, ATTN_W, D_MODEL), lambda i: (l, 1, 0)),
            pl.BlockSpec((TM, D_MODEL), row),
            pl.BlockSpec((None, 6, D_MODEL), lambda i: (i // tiles_per_group, 0, 0)),
            pl.BlockSpec((1, D_MODEL), const),
            pl.BlockSpec((1, D_MODEL), const),
        ],
        out_specs=pl.BlockSpec((TM, D_MODEL), row),
        out_shape=jax.ShapeDtypeStruct((n, D_MODEL), F32),
        compiler_params=_cparams(("parallel",)),
        name="out_proj",
    )(o_gdn, o_att, w_o, w_o, x, mod, ln_g, ln_b)


def _ffn_up_kernel(x_ref, m_ref, wg_ref, wu_ref, o_ref, h_ref):
    @pl.when(pl.program_id(1) == 0)
    def _():
        h_ref[...] = (x_ref[...] * (1.0 + m_ref[4:5, :]) + m_ref[3:4, :]).astype(BF16)

    h = h_ref[...]
    gate = _dot(h, wg_ref[...])
    up = _dot(h, wu_ref[...])
    o_ref[...] = (_silu(gate) * up).astype(o_ref.dtype)


def _ffn_up(x, mod, w_gu, l, rows_per_group):
    n = x.shape[0]
    nj = D_FF // TN_FF
    tm = min(TM_UP, rows_per_group)
    tiles_per_group = rows_per_group // tm
    return pl.pallas_call(
        _ffn_up_kernel,
        grid=(n // tm, nj),
        in_specs=[
            pl.BlockSpec((tm, D_MODEL), lambda i, j: (i, 0)),
            pl.BlockSpec((None, 6, D_MODEL), lambda i, j: (i // tiles_per_group, 0, 0)),
            pl.BlockSpec((None, D_MODEL, TN_FF), lambda i, j: (l, 0, j)),
            pl.BlockSpec((None, D_MODEL, TN_FF), lambda i, j: (l, 0, nj + j)),
        ],
        out_specs=pl.BlockSpec((tm, TN_FF), lambda i, j: (i, j)),
        out_shape=jax.ShapeDtypeStruct((n, D_FF), BF16),
        scratch_shapes=[pltpu.VMEM((tm, D_MODEL), BF16)],
        compiler_params=_cparams(("parallel", "arbitrary")),
        name="ffn_up",
    )(x, mod, w_gu, w_gu)


def _ffn_down_kernel(a_ref, w_ref, x_ref, m_ref, g_ref, b_ref, o_ref, acc_ref):
    k = pl.program_id(1)

    part = _dot(a_ref[...], w_ref[...])

    last = pl.num_programs(1) - 1

    @pl.when(k == 0)
    def _():
        acc_ref[...] = part

    @pl.when((k > 0) & (k < last))
    def _():
        acc_ref[...] += part

    @pl.when(k == last)
    def _():
        y = DN_ALPHA * x_ref[...] + m_ref[5:6, :] * (acc_ref[...] + part)
        o_ref[...] = _layer_norm(y, g_ref[...], b_ref[...])


def _ffn_down(act, w_down, l, x, mod, ln_g, ln_b, tiles_per_group):
    n = x.shape[0]
    return pl.pallas_call(
        _ffn_down_kernel,
        grid=(n // TM, D_FF // TK_FF),
        in_specs=[
            pl.BlockSpec((TM, TK_FF), lambda i, k: (i, k)),
            pl.BlockSpec((None, TK_FF, D_MODEL), lambda i, k: (l, k, 0)),
            pl.BlockSpec((TM, D_MODEL), lambda i, k: (i, 0)),
            pl.BlockSpec((None, 6, D_MODEL), lambda i, k: (i // tiles_per_group, 0, 0)),
            pl.BlockSpec((1, D_MODEL), lambda i, k: (0, 0)),
            pl.BlockSpec((1, D_MODEL), lambda i, k: (0, 0)),
        ],
        out_specs=pl.BlockSpec((TM, D_MODEL), lambda i, k: (i, 0)),
        out_shape=jax.ShapeDtypeStruct((n, D_MODEL), F32),
        scratch_shapes=[pltpu.VMEM((TM, D_MODEL), F32)],
        compiler_params=_cparams(("parallel", "arbitrary")),
        name="ffn_down",
    )(act, w_down, x, mod, ln_g, ln_b)


def _trunk_layer(x, mod, lw, l, bsz, t, latent, ctx_out):
    assert t % TM == 0 or TM % t == 0
    tiles_per_group = max(t // TM, 1) if latent is not None else x.shape[0] // TM
    proj = _in_proj(x, mod, lw["w_in"], l, tiles_per_group)
    if latent is None:
        kc_all, vc_all, s_all = ctx_out
        o_gdn, s_all = _gdn(proj, lw["conv_w"], lw["a_row"], lw["d_row"], lw["gdn_norm_w"], None, s_all, l, bsz, t)
        o_att, kc_all, vc_all = _attn_ctx(proj, lw["q_norm_w"], lw["k_norm_w"], kc_all, vc_all, l, bsz, t)
        ctx_out = (kc_all, vc_all, s_all)
    else:
        cache_k, cache_v, state_gdn, cos_t, sin_t = latent
        (o_gdn,) = _gdn(proj, lw["conv_w"], lw["a_row"], lw["d_row"], lw["gdn_norm_w"], state_gdn, None, l, bsz, t)
        o_att = _attn_lat(proj, cache_k, cache_v, lw["q_norm_w"], lw["k_norm_w"], cos_t, sin_t, l, bsz, t, 256)
    x1 = _out_proj(o_gdn, o_att, lw["w_o"], l, x, mod, lw["ln1_g"], lw["ln1_b"], tiles_per_group)
    act = _ffn_up(x1, mod, lw["w_gate_up"], l, tiles_per_group * TM)
    x2 = _ffn_down(act, lw["w_down"], l, x1, mod, lw["ln2_g"], lw["ln2_b"], tiles_per_group)
    return x2, ctx_out


def kernel(x_prompt, x_sample, cache_k, cache_v, state_gdn, c, c_ctx, w_ada, b_ada, w_in, conv_w, a_log, dt_bias,
           gdn_norm_w, q_norm_w, k_norm_w, w_o, ln1_g, ln1_b, ln2_g, ln2_b, w_gate_up, w_down):
    bsz, seq, _ = x_prompt.shape
    dbsz, dseq, _ = x_sample.shape
    past = cache_k.shape[2]

    i_b = QKV_W + GDN_W
    i_q = i_b + 4 * H_GDN
    w_in_r = jnp.concatenate(
        [w_in[:, :, :i_b], w_in[:, :, i_q:], w_in[:, :, i_b:i_q],
         jnp.zeros((DEPTH, D_MODEL, LANE - 4 * H_GDN), w_in.dtype)], axis=-1).astype(BF16)
    w_o_b = w_o.astype(BF16)
    w_gu_b = w_gate_up.astype(BF16)
    w_down_b = w_down.astype(BF16)

    cond8 = jnp.concatenate([c_ctx[None, :], c, jnp.zeros((8 - 1 - dbsz, D_MODEL), F32)], axis=0)
    mod = _ada(cond8, w_ada, b_ada.reshape(DEPTH, 1, 6 * D_MODEL)).reshape(DEPTH, 8, 6, D_MODEL)

    cos_t, sin_t = _rope_lane_tables(dseq)
    ck = cache_k.reshape(dbsz, DEPTH, past, KV_W)
    cv = cache_v.reshape(dbsz, DEPTH, past, KV_W)

    def gate_rows(p):
        return jnp.zeros((1, LANE), F32).at[0, 2 * H_GDN:4 * H_GDN].set(p.reshape(-1))

    xp = x_prompt.reshape(bsz * seq, D_MODEL)
    xs = x_sample.reshape(dbsz * dseq, D_MODEL)
    ctx_out = (jnp.zeros((bsz, DEPTH, seq, KV_W), F32), jnp.zeros((bsz, DEPTH, seq, KV_W), F32),
               jnp.zeros((bsz, DEPTH, 2, H_GDN, DK, DK), F32))
    for l in range(DEPTH):
        lw = {
            "w_in": w_in_r, "conv_w": conv_w[l], "a_row": gate_rows(a_log[l]), "d_row": gate_rows(dt_bias[l]),
            "gdn_norm_w": gdn_norm_w[l].reshape(1, DK), "q_norm_w": q_norm_w[l].reshape(1, HEAD_DIM),
            "k_norm_w": k_norm_w[l].reshape(1, HEAD_DIM), "w_o": w_o_b,
            "ln1_g": ln1_g[l].reshape(1, D_MODEL), "ln1_b": ln1_b[l].reshape(1, D_MODEL),
            "ln2_g": ln2_g[l].reshape(1, D_MODEL), "ln2_b": ln2_b[l].reshape(1, D_MODEL),
            "w_gate_up": w_gu_b, "w_down": w_down_b,
        }
        xp, ctx_out = _trunk_layer(xp, mod[l, 0:1], lw, l, bsz, seq, None, ctx_out)
        xs, _ = _trunk_layer(xs, mod[l, 1:1 + dbsz], lw, l, dbsz, dseq, (ck, cv, state_gdn, cos_t, sin_t), None)
    kc_all, vc_all, s_all = ctx_out
    return (xp.reshape(bsz, seq, D_MODEL), xs.reshape(dbsz, dseq, D_MODEL),
            kc_all.reshape(bsz, DEPTH, seq, N_KV, HEAD_DIM), vc_all.reshape(bsz, DEPTH, seq, N_KV, HEAD_DIM), s_all)
```

```python
import functools
import math

import jax
import jax.numpy as jnp
from jax import lax
from jax.experimental import pallas as pl
from jax.experimental.pallas import tpu as pltpu

D_MODEL = 2048
DEPTH = 2
GRID_W = 64
DK = 128
H_GDN = 8
GDN_W = H_GDN * DK
HEAD_DIM = 128
N_Q = 8
N_KV = 2
GRP = N_Q // N_KV
ATTN_W = N_Q * HEAD_DIM
KV_W = N_KV * HEAD_DIM
CONV_K = 5
D_FF = 5632
ROPE_BASE = 10000.0
RMS_EPS = 1e-6
LN_EPS = 1e-5
DN_ALPHA = (2 * DEPTH) ** 0.25
QKV_W = 3 * GDN_W
IN_W = QKV_W + GDN_W + 4 * H_GDN + ATTN_W + 2 * KV_W

LANE = 128
GDN_CHUNK = 128
GDN_HB = 2
GDN_HB_SHORT = 4
GDN_SHORT_T = 256
GDN_CG = 2
CONV_PAD = 8
COL_Z = QKV_W
COL_Q = COL_Z + GDN_W
COL_K = COL_Q + ATTN_W
COL_V = COL_K + KV_W
COL_BA = COL_V + KV_W
IN_W_PAD = COL_BA + LANE

TM = 512
TM_UP = 1024
TN_IN = 1920
TN_FF = 512
TK_FF = 2816
TN_ADA = 1024
VMEM_LIMIT = 56 * 1024 * 1024

F32 = jnp.float32
BF16 = jnp.bfloat16


def _cparams(sem):
    return pltpu.CompilerParams(dimension_semantics=sem, vmem_limit_bytes=VMEM_LIMIT)


def _dot(a, b):
    return jnp.dot(a, b, preferred_element_type=F32)


def _dot_nt(a, b):
    return lax.dot_general(a, b, (((1,), (1,)), ((), ())), preferred_element_type=F32)


def _dot_tn(a, b):
    return lax.dot_general(a, b, (((0,), (0,)), ((), ())), preferred_element_type=F32)


def _sigmoid(x):
    return 1.0 / (1.0 + jnp.exp(-x))


def _silu(x):
    return x * _sigmoid(x)


def _layer_norm(y, g, b):
    mu = jnp.mean(y, axis=-1, keepdims=True)
    yc = y - mu
    var = jnp.mean(yc * yc, axis=-1, keepdims=True)
    return yc * lax.rsqrt(var + LN_EPS) * g + b


def _rms_norm(x, w):
    return x * lax.rsqrt(jnp.mean(x * x, axis=-1, keepdims=True) + RMS_EPS) * w


def _ada_kernel(c_ref, w_ref, b_ref, o_ref):
    c = c_ref[...]
    o_ref[...] = _dot(_silu(c).astype(BF16), w_ref[...].astype(BF16)) + b_ref[...]


def _ada(cond8, w_ada, b_ada):
    n = w_ada.shape[-1]
    return pl.pallas_call(
        _ada_kernel,
        grid=(DEPTH, n // TN_ADA),
        in_specs=[
            pl.BlockSpec((8, D_MODEL), lambda l, j: (0, 0)),
            pl.BlockSpec((None, D_MODEL, TN_ADA), lambda l, j: (l, 0, j)),
            pl.BlockSpec((None, 1, TN_ADA), lambda l, j: (l, 0, j)),
        ],
        out_specs=pl.BlockSpec((None, 8, TN_ADA), lambda l, j: (l, 0, j)),
        out_shape=jax.ShapeDtypeStruct((DEPTH, 8, n), F32),
        compiler_params=_cparams(("arbitrary", "arbitrary")),
        name="ada",
    )(cond8, w_ada, b_ada)


def _inproj_kernel(x_ref, m_ref, w_ref, o_ref, h_ref):
    @pl.when(pl.program_id(1) == 0)
    def _():
        h_ref[...] = (x_ref[...] * (1.0 + m_ref[1:2, :]) + m_ref[0:1, :]).astype(BF16)

    o_ref[...] = _dot(h_ref[...], w_ref[...])


def _in_proj(x, mod, w, l, tiles_per_group):
    n = x.shape[0]
    return pl.pallas_call(
        _inproj_kernel,
        grid=(n // TM, IN_W_PAD // TN_IN),
        in_specs=[
            pl.BlockSpec((TM, D_MODEL), lambda i, j: (i, 0)),
            pl.BlockSpec((None, 6, D_MODEL), lambda i, j: (i // tiles_per_group, 0, 0)),
            pl.BlockSpec((None, D_MODEL, TN_IN), lambda i, j: (l, 0, j)),
        ],
        out_specs=pl.BlockSpec((TM, TN_IN), lambda i, j: (i, j)),
        out_shape=jax.ShapeDtypeStruct((n, IN_W_PAD), F32),
        scratch_shapes=[pltpu.VMEM((TM, D_MODEL), BF16)],
        compiler_params=_cparams(("parallel", "arbitrary")),
        name="in_proj",
    )(x, mod, w)


def _conv_silu(xp, i, t, w_ref, cols):
    acc = None
    for j in range(CONV_K):
        start = CONV_PAD - CONV_K // 2 + j
        term = xp[i, start:start + t, :] * w_ref[j:j + 1, cols]
        acc = term if acc is None else acc + term
    return _silu(acc)


def _seg_cumsum(x, reverse):
    t = x.shape[0]
    pos = lax.broadcasted_iota(jnp.int32, x.shape, 0) % GDN_CHUNK
    s = 1
    while s < GDN_CHUNK:
        if reverse:
            x = x + jnp.where(pos < GDN_CHUNK - s, pltpu.roll(x, shift=t - s, axis=0), 0.0)
        else:
            x = x + jnp.where(pos >= s, pltpu.roll(x, shift=s, axis=0), 0.0)
        s *= 2
    return x


def _unit_tri_inverses(lms):
    c = lms[0].shape[0]
    row = lax.broadcasted_iota(jnp.int32, (c, c), 0)
    col = lax.broadcasted_iota(jnp.int32, (c, c), 1)
    rc = row ^ col
    eye = (row == col).astype(F32)
    xs = None
    for lvl in range(int(math.log2(c))):
        keep = (rc >> lvl) == 1
        if xs is None:
            xs = [eye - jnp.where(keep, lm, 0.0) for lm in lms]
            lbs = [lm.astype(BF16) for lm in lms]
            zero = jnp.zeros((c, c), BF16)
            continue
        xbs = [x.astype(BF16) for x in xs]
        ys = [_dot(xb, jnp.where(keep, lb, zero)).astype(BF16) for xb, lb in zip(xbs, lbs)]
        xs = [x - _dot(y, xb) for x, y, xb in zip(xs, ys, xbs)]
    return xs


def _gdn_intra(qs, ks, vs, gs, bs, us, wb, qb, ab, kd, chunks):
    c = GDN_CHUNK
    row = lax.broadcasted_iota(jnp.int32, (c, c), 0)
    col = lax.broadcasted_iota(jnp.int32, (c, c), 1)
    incl = (row >= col, row <= col)
    strict = (row > col, row < col)
    kqs = []
    for j, r0 in chunks:
        rows = pl.ds(r0, c)
        kcb = ks[j, rows, :].astype(BF16)
        kqs.append(_dot_nt(jnp.concatenate([kcb, qs[j, rows, :].astype(BF16)], axis=0), kcb))
    work, lms = [], []
    for (j, r0), kq in zip(chunks, kqs):
        rows = pl.ds(r0, c)
        for d in range(2):
            gcum = gs[j, d, rows, :]
            beta = bs[j, d, rows, :]
            dm = gcum - gcum.T
            decay = jnp.where(incl[d], jnp.exp(jnp.where(incl[d], dm, 0.0)), 0.0)
            lms.append(jnp.where(strict[d], kq[:c] * beta * decay, 0.0))
            ab[j, d, rows, :] = (kq[c:] * decay).astype(BF16)
            work.append((j, d, rows, gcum, beta))
    xs = _unit_tri_inverses(lms)
    for (j, d, rows, gcum, beta), x in zip(work, xs):
        qc = qs[j, rows, :]
        kc = ks[j, rows, :]
        egc = jnp.exp(gcum)
        rhs = jnp.concatenate([vs[j, rows, :] * beta, kc * (beta * egc)], axis=1)
        sol = _dot(x.astype(BF16), rhs.astype(BF16))
        g_last = gcum[c - 1:c, :] if d == 0 else gcum[0:1, :]
        us[j, d, rows, :] = sol[:, :DK]
        wb[j, d, rows, :] = sol[:, DK:].astype(BF16)
        qb[j, d, rows, :] = (qc * egc).astype(BF16)
        kd[j, d, rows, :] = (kc * jnp.exp(g_last - gcum)).astype(BF16)


def _gdn_scan_step(gs, us, wb, qb, ab, kd, ss, os, j, d, r0, zero_state):
    c = GDN_CHUNK
    rows = pl.ds(r0, c)
    if zero_state:
        v_new = us[j, d, rows, :].astype(BF16)
        os[j, rows, :] += _dot(ab[j, d, rows, :], v_new)
        ss[j, d] = _dot_tn(kd[j, d, rows, :], v_new)
        return
    s_state = ss[j, d]
    ws_qs = _dot(jnp.concatenate([wb[j, d, rows, :], qb[j, d, rows, :]], axis=0), s_state.astype(BF16))
    v_new = (us[j, d, rows, :] - ws_qs[:c]).astype(BF16)
    os[j, rows, :] += ws_qs[c:] + _dot(ab[j, d, rows, :], v_new)
    g_last = gs[j, d, pl.ds(r0 + (c - 1 if d == 0 else 0), 1), :]
    ss[j, d] = s_state * jnp.exp(g_last) + _dot_tn(kd[j, d, rows, :], v_new)


def _gdn_kernel(q_ref, k_ref, v_ref, z_ref, ba_ref, arow_ref, drow_ref, cwq_ref, cwk_ref, cwv_ref, nw_ref, *rest,
                has_s0, has_sout):
    rest = list(rest)
    s0_ref = rest.pop(0) if has_s0 else None
    if has_sout:
        rest.pop(0)
    o_ref = rest.pop(0)
    sout_ref = rest.pop(0) if has_sout else None
    xp, qs, ks, vs, gate, gs, bs, us, wb, qb, ab, kd, ss, os = rest
    hp = pl.program_id(1)
    t = q_ref.shape[0]
    hb = q_ref.shape[1] // DK
    lane = lax.broadcasted_iota(jnp.int32, (t, LANE), 1)

    @pl.when(hp == 0)
    def _():
        ba = ba_ref[...]
        x = ba + drow_ref[...]
        softplus = jnp.maximum(x, 0.0) + jnp.log(1.0 + jnp.exp(-jnp.abs(x)))
        g = -jnp.exp(arow_ref[...]) * softplus
        gcum = jnp.where(lane >= 3 * H_GDN, _seg_cumsum(g, True), _seg_cumsum(g, False))
        gate[...] = jnp.where(lane < 2 * H_GDN, _sigmoid(ba), gcum)

    pad0 = jnp.zeros((CONV_PAD, DK), F32)
    for i, ref in enumerate((q_ref, k_ref, v_ref)):
        for j in range(hb):
            xp[i * hb + j, 0:CONV_PAD, :] = pad0
            xp[i * hb + j, CONV_PAD + t:, :] = pad0
            xp[i * hb + j, CONV_PAD:CONV_PAD + t, :] = ref[:, j * DK:(j + 1) * DK]

    for j in range(hb):
        h = hp * hb + j
        sl = slice(j * DK, (j + 1) * DK)
        q = _conv_silu(xp, j, t, cwq_ref, sl)
        qs[j] = q * lax.rsqrt(jnp.sum(q * q, axis=-1, keepdims=True) + RMS_EPS) * (DK ** -0.5)
        k = _conv_silu(xp, hb + j, t, cwk_ref, sl)
        ks[j] = k * lax.rsqrt(jnp.sum(k * k, axis=-1, keepdims=True) + RMS_EPS)
        vs[j] = _conv_silu(xp, 2 * hb + j, t, cwv_ref, sl)
        gt = gate[...]
        for d in range(2):
            bcol = jnp.sum(jnp.where(lane == d * H_GDN + h, gt, 0.0), axis=-1, keepdims=True)
            gcol = jnp.sum(jnp.where(lane == (2 + d) * H_GDN + h, gt, 0.0), axis=-1, keepdims=True)
            bs[j, d] = jnp.broadcast_to(bcol, (t, LANE))
            gs[j, d] = jnp.broadcast_to(gcol, (t, LANE))
            if has_s0:
                ss[j, d] = s0_ref[d, j]
        os[j] = jnp.zeros((t, DK), F32)

    n_chunks = t // GDN_CHUNK
    n_groups = n_chunks // GDN_CG

    def intra_group(i):
        chunks = []
        for j in range(hb):
            for cc in range(GDN_CG):
                r0 = (i * GDN_CG + cc) * GDN_CHUNK
                if not isinstance(r0, int):
                    r0 = pl.multiple_of(r0, GDN_CHUNK)
                chunks.append((j, r0))
        _gdn_intra(qs, ks, vs, gs, bs, us, wb, qb, ab, kd, chunks)

    def scan_step(i, first=False):
        for j in range(hb):
            for d in range(2):
                r0 = (i if d == 0 else n_chunks - 1 - i) * GDN_CHUNK
                if not isinstance(r0, int):
                    r0 = pl.multiple_of(r0, GDN_CHUNK)
                _gdn_scan_step(gs, us, wb, qb, ab, kd, ss, os, j, d, r0, zero_state=first and not has_s0)

    if n_groups == 1:
        intra_group(0)
    else:
        lax.fori_loop(0, n_groups, lambda i, c: (intra_group(i), c)[1], 0)
    scan_step(0, first=True)
    if n_chunks <= 2:
        for i in range(1, n_chunks):
            scan_step(i)
    else:
        lax.fori_loop(1, n_chunks, lambda i, c: (scan_step(i), c)[1], 0)

    for j in range(hb):
        sl = slice(j * DK, (j + 1) * DK)
        o_ref[:, sl] = (_rms_norm(os[j], nw_ref[...]) * _silu(z_ref[:, sl])).astype(o_ref.dtype)
        if has_sout:
            for d in range(2):
                sout_ref[d, j] = ss[j, d]


def _gdn(proj, conv_w, a_row, d_row, norm_w, s0, s_all, l, bsz, t):
    has_s0 = s0 is not None
    want_state = s_all is not None
    hb = GDN_HB_SHORT if t <= GDN_SHORT_T else GDN_HB
    wblk = hb * DK
    col = lambda off: (lambda b, hp: (b, off // wblk + hp))
    row0 = lambda b, hp: (0, 0)
    in_specs = [
        pl.BlockSpec((t, wblk), col(0)),
        pl.BlockSpec((t, wblk), col(GDN_W)),
        pl.BlockSpec((t, wblk), col(2 * GDN_W)),
        pl.BlockSpec((t, wblk), col(COL_Z)),
        pl.BlockSpec((t, LANE), lambda b, hp: (b, COL_BA // LANE)),
        pl.BlockSpec((1, LANE), row0),
        pl.BlockSpec((1, LANE), row0),
        pl.BlockSpec((CONV_K, wblk), lambda b, hp: (0, hp)),
        pl.BlockSpec((CONV_K, wblk), lambda b, hp: (0, GDN_W // wblk + hp)),
        pl.BlockSpec((CONV_K, wblk), lambda b, hp: (0, 2 * GDN_W // wblk + hp)),
        pl.BlockSpec((1, DK), row0),
    ]
    args = [proj, proj, proj, proj, proj, a_row, d_row, conv_w, conv_w, conv_w, norm_w]
    if has_s0:
        in_specs.append(pl.BlockSpec((None, None, 2, hb, DK, DK), lambda b, hp: (b, l, 0, hp, 0, 0)))
        args.append(s0)
    out_specs = [pl.BlockSpec((t, wblk), lambda b, hp: (b, hp))]
    out_shape = [jax.ShapeDtypeStruct((bsz * t, GDN_W), BF16)]
    aliases = {}
    if want_state:
        aliases = {len(args): 1}
        in_specs.append(pl.BlockSpec(memory_space=pl.ANY))
        args.append(s_all)
        out_specs.append(pl.BlockSpec((None, None, 2, hb, DK, DK), lambda b, hp: (b, l, 0, hp, 0, 0)))
        out_shape.append(jax.ShapeDtypeStruct(s_all.shape, F32))
    per_head = lambda dt: pltpu.VMEM((hb, t, DK), dt)
    per_dir = lambda dt: pltpu.VMEM((hb, 2, t, DK), dt)
    return pl.pallas_call(
        functools.partial(_gdn_kernel, has_s0=has_s0, has_sout=want_state),
        grid=(bsz, H_GDN // hb),
        in_specs=in_specs,
        out_specs=out_specs,
        out_shape=out_shape,
        input_output_aliases=aliases,
        scratch_shapes=[
            pltpu.VMEM((3 * hb, t + 2 * CONV_PAD, DK), F32),
            per_head(F32), per_head(F32), per_head(F32),
            pltpu.VMEM((t, LANE), F32),
            per_dir(F32), per_dir(F32),
            per_dir(F32), per_dir(BF16), per_dir(BF16), per_dir(BF16), per_dir(BF16),
            pltpu.VMEM((hb, 2, DK, DK), F32),
            per_head(F32),
        ],
        compiler_params=_cparams(("parallel", "arbitrary")),
        name="gdn_lat" if has_s0 else "gdn_ctx",
    )(*args)


def _rope(x, cos, sin):
    lane = lax.broadcasted_iota(jnp.int32, x.shape, 1)
    quarter = HEAD_DIM // 4
    partner = jnp.where((lane % (2 * quarter)) < quarter,
                        pltpu.roll(x, shift=HEAD_DIM - quarter, axis=1),
                        pltpu.roll(x, shift=quarter, axis=1))
    return x * cos + partner * sin


def _softmax_pv(score_blocks, value_blocks):
    m = None
    for s in score_blocks:
        bm = jnp.max(s, axis=-1, keepdims=True)
        m = bm if m is None else jnp.maximum(m, bm)
    den, acc = None, None
    for s, v in zip(score_blocks, value_blocks):
        p = jnp.exp(s - m)
        ps = jnp.sum(p, axis=-1, keepdims=True)
        pv = _dot(p.astype(BF16), v)
        den = ps if den is None else den + ps
        acc = pv if acc is None else acc + pv
    return acc / den


def _attn_ctx_kernel(q_ref, k_ref, v_ref, qw_ref, kw_ref, kc_all_ref, vc_all_ref, o_ref, kc_ref, vc_ref):
    del kc_all_ref, vc_all_ref
    scale = HEAD_DIM ** -0.5
    vc_ref[...] = v_ref[...]
    for g in range(N_KV):
        gl = slice(g * HEAD_DIM, (g + 1) * HEAD_DIM)
        kg = _rms_norm(k_ref[:, gl], kw_ref[...])
        kc_ref[:, gl] = kg
        kgb = kg.astype(BF16)
        vgb = v_ref[:, gl].astype(BF16)
        for hh in range(GRP):
            hl = slice((g * GRP + hh) * HEAD_DIM, (g * GRP + hh + 1) * HEAD_DIM)
            qh = _rms_norm(q_ref[:, hl], qw_ref[...]).astype(BF16)
            s = _dot_nt(qh, kgb) * scale
            o_ref[:, hl] = _softmax_pv([s], [vgb]).astype(o_ref.dtype)


def _attn_ctx(proj, q_norm_w, k_norm_w, kc_all, vc_all, l, bsz, t):
    cache_spec = pl.BlockSpec((None, None, t, KV_W), lambda b: (b, l, 0, 0))
    return pl.pallas_call(
        _attn_ctx_kernel,
        grid=(bsz,),
        in_specs=[
            pl.BlockSpec((t, ATTN_W), lambda b: (b, COL_Q // ATTN_W)),
            pl.BlockSpec((t, KV_W), lambda b: (b, COL_K // KV_W)),
            pl.BlockSpec((t, KV_W), lambda b: (b, COL_V // KV_W)),
            pl.BlockSpec((1, HEAD_DIM), lambda b: (0, 0)),
            pl.BlockSpec((1, HEAD_DIM), lambda b: (0, 0)),
            pl.BlockSpec(memory_space=pl.ANY),
            pl.BlockSpec(memory_space=pl.ANY),
        ],
        out_specs=[pl.BlockSpec((t, ATTN_W), lambda b: (b, 0)), cache_spec, cache_spec],
        out_shape=[
            jax.ShapeDtypeStruct((bsz * t, ATTN_W), BF16),
            jax.ShapeDtypeStruct(kc_all.shape, F32),
            jax.ShapeDtypeStruct(vc_all.shape, F32),
        ],
        input_output_aliases={5: 1, 6: 2},
        compiler_params=_cparams(("parallel",)),
        name="attn_ctx",
    )(proj, proj, proj, q_norm_w, k_norm_w, kc_all, vc_all)


def _attn_lat_kernel(q_ref, k_ref, v_ref, ck_ref, cv_ref, qw_ref, kw_ref, cq_ref, sq_ref, ck_cos_ref, ck_sin_ref,
                     o_ref):
    scale = HEAD_DIM ** -0.5
    for g in range(N_KV):
        gl = slice(g * HEAD_DIM, (g + 1) * HEAD_DIM)
        kg = _rope(_rms_norm(k_ref[:, gl], kw_ref[...]), ck_cos_ref[...], ck_sin_ref[...]).astype(BF16)
        vg = v_ref[:, gl].astype(BF16)
        ckg = ck_ref[:, gl].astype(BF16)
        cvg = cv_ref[:, gl].astype(BF16)
        for hh in range(GRP):
            hl = slice((g * GRP + hh) * HEAD_DIM, (g * GRP + hh + 1) * HEAD_DIM)
            qh = _rope(_rms_norm(q_ref[:, hl], qw_ref[...]), cq_ref[...], sq_ref[...]).astype(BF16)
            s1 = _dot_nt(qh, kg) * scale
            s2 = _dot_nt(qh, ckg) * scale
            o_ref[:, hl] = _softmax_pv([s1, s2], [vg, cvg]).astype(o_ref.dtype)


def _attn_lat(proj, cache_k, cache_v, q_norm_w, k_norm_w, cos_t, sin_t, l, bsz, t, tq):
    past = cache_k.shape[2]
    nq = t // tq
    return pl.pallas_call(
        _attn_lat_kernel,
        grid=(bsz, nq),
        in_specs=[
            pl.BlockSpec((tq, ATTN_W), lambda b, i: (b * nq + i, COL_Q // ATTN_W)),
            pl.BlockSpec((t, KV_W), lambda b, i: (b, COL_K // KV_W)),
            pl.BlockSpec((t, KV_W), lambda b, i: (b, COL_V // KV_W)),
            pl.BlockSpec((None, None, past, KV_W), lambda b, i: (b, l, 0, 0)),
            pl.BlockSpec((None, None, past, KV_W), lambda b, i: (b, l, 0, 0)),
            pl.BlockSpec((1, HEAD_DIM), lambda b, i: (0, 0)),
            pl.BlockSpec((1, HEAD_DIM), lambda b, i: (0, 0)),
            pl.BlockSpec((tq, HEAD_DIM), lambda b, i: (i, 0)),
            pl.BlockSpec((tq, HEAD_DIM), lambda b, i: (i, 0)),
            pl.BlockSpec((t, HEAD_DIM), lambda b, i: (0, 0)),
            pl.BlockSpec((t, HEAD_DIM), lambda b, i: (0, 0)),
        ],
        out_specs=pl.BlockSpec((tq, ATTN_W), lambda b, i: (b * nq + i, 0)),
        out_shape=jax.ShapeDtypeStruct((bsz * t, ATTN_W), BF16),
        compiler_params=_cparams(("parallel", "arbitrary")),
        name="attn_lat",
    )(proj, proj, proj, cache_k, cache_v, q_norm_w, k_norm_w, cos_t, sin_t, cos_t, sin_t)


def _rope_lane_tables(n_tok):
    rows = n_tok // GRID_W
    row = jnp.repeat(jnp.arange(rows), GRID_W).astype(F32)
    colp = jnp.tile(jnp.arange(GRID_W), rows).astype(F32)
    half = HEAD_DIM // 2
    inv_freq = ROPE_BASE ** (-jnp.arange(0, half, 2, dtype=F32) / half)
    ang_r = row[:, None] * inv_freq
    ang_c = colp[:, None] * inv_freq
    cos_t = jnp.concatenate([jnp.cos(ang_r), jnp.cos(ang_r), jnp.cos(ang_c), jnp.cos(ang_c)], axis=-1)
    sin_t = jnp.concatenate([-jnp.sin(ang_r), jnp.sin(ang_r), -jnp.sin(ang_c), jnp.sin(ang_c)], axis=-1)
    return cos_t, sin_t


def _outproj_kernel(og_ref, oa_ref, wg_ref, wa_ref, x_ref, m_ref, g_ref, b_ref, o_ref):
    mix = _dot(og_ref[...], wg_ref[...]) + _dot(oa_ref[...], wa_ref[...])
    y = DN_ALPHA * x_ref[...] + m_ref[2:3, :] * mix
    o_ref[...] = _layer_norm(y, g_ref[...], b_ref[...])


def _out_proj(o_gdn, o_att, w_o, l, x, mod, ln_g, ln_b, tiles_per_group):
    n = x.shape[0]
    row = lambda i: (i, 0)
    const = lambda i: (0, 0)
    return pl.pallas_call(
        _outproj_kernel,
        grid=(n // TM,),
        in_specs=[
            pl.BlockSpec((TM, GDN_W), row),
            pl.BlockSpec((TM, ATTN_W), row),
            pl.BlockSpec((None, GDN_W, D_MODEL), lambda i: (l, 0, 0)),
            pl.BlockSpec((None, ATTN_W, D_MODEL), lambda i: (l, 1, 0)),
            pl.BlockSpec((TM, D_MODEL), row),
            pl.BlockSpec((None, 6, D_MODEL), lambda i: (i // tiles_per_group, 0, 0)),
            pl.BlockSpec((1, D_MODEL), const),
            pl.BlockSpec((1, D_MODEL), const),
        ],
        out_specs=pl.BlockSpec((TM, D_MODEL), row),
        out_shape=jax.ShapeDtypeStruct((n, D_MODEL), F32),
        compiler_params=_cparams(("parallel",)),
        name="out_proj",
    )(o_gdn, o_att, w_o, w_o, x, mod, ln_g, ln_b)


def _ffn_up_kernel(x_ref, m_ref, wg_ref, wu_ref, o_ref, h_ref):
    @pl.when(pl.program_id(1) == 0)
    def _():
        h_ref[...] = (x_ref[...] * (1.0 + m_ref[4:5, :]) + m_ref[3:4, :]).astype(BF16)

    h = h_ref[...]
    gate = _dot(h, wg_ref[...].astype(BF16))
    up = _dot(h, wu_ref[...].astype(BF16))
    o_ref[...] = (_silu(gate) * up).astype(o_ref.dtype)


def _ffn_up(x, mod, w_gu, l, rows_per_group):
    n = x.shape[0]
    nj = D_FF // TN_FF
    tm = min(TM_UP, rows_per_group)
    tiles_per_group = rows_per_group // tm
    return pl.pallas_call(
        _ffn_up_kernel,
        grid=(n // tm, nj),
        in_specs=[
            pl.BlockSpec((tm, D_MODEL), lambda i, j: (i, 0)),
            pl.BlockSpec((None, 6, D_MODEL), lambda i, j: (i // tiles_per_group, 0, 0)),
            pl.BlockSpec((None, D_MODEL, TN_FF), lambda i, j: (l, 0, j)),
            pl.BlockSpec((None, D_MODEL, TN_FF), lambda i, j: (l, 0, nj + j)),
        ],
        out_specs=pl.BlockSpec((tm, TN_FF), lambda i, j: (i, j)),
        out_shape=jax.ShapeDtypeStruct((n, D_FF), BF16),
        scratch_shapes=[pltpu.VMEM((tm, D_MODEL), BF16)],
        compiler_params=_cparams(("parallel", "arbitrary")),
        name="ffn_up",
    )(x, mod, w_gu, w_gu)


def _ffn_down_kernel(a_ref, w_ref, x_ref, m_ref, g_ref, b_ref, o_ref, acc_ref):
    k = pl.program_id(1)

    part = _dot(a_ref[...], w_ref[...])

    last = pl.num_programs(1) - 1

    @pl.when(k == 0)
    def _():
        acc_ref[...] = part

    @pl.when((k > 0) & (k < last))
    def _():
        acc_ref[...] += part

    @pl.when(k == last)
    def _():
        y = DN_ALPHA * x_ref[...] + m_ref[5:6, :] * (acc_ref[...] + part)
        o_ref[...] = _layer_norm(y, g_ref[...], b_ref[...])


def _ffn_down(act, w_down, l, x, mod, ln_g, ln_b, tiles_per_group):
    n = x.shape[0]
    return pl.pallas_call(
        _ffn_down_kernel,
        grid=(n // TM, D_FF // TK_FF),
        in_specs=[
            pl.BlockSpec((TM, TK_FF), lambda i, k: (i, k)),
            pl.BlockSpec((None, TK_FF, D_MODEL), lambda i, k: (l, k, 0)),
            pl.BlockSpec((TM, D_MODEL), lambda i, k: (i, 0)),
            pl.BlockSpec((None, 6, D_MODEL), lambda i, k: (i // tiles_per_group, 0, 0)),
            pl.BlockSpec((1, D_MODEL), lambda i, k: (0, 0)),
            pl.BlockSpec((1, D_MODEL), lambda i, k: (0, 0)),
        ],
        out_specs=pl.BlockSpec((TM, D_MODEL), lambda i, k: (i, 0)),
        out_shape=jax.ShapeDtypeStruct((n, D_MODEL), F32),
        scratch_shapes=[pltpu.VMEM((TM, D_MODEL), F32)],
        compiler_params=_cparams(("parallel", "arbitrary")),
        name="ffn_down",
    )(act, w_down, x, mod, ln_g, ln_b)


def _trunk_layer(x, mod, lw, l, bsz, t, latent, ctx_out):
    assert t % TM == 0 or TM % t == 0
    tiles_per_group = max(t // TM, 1) if latent is not None else x.shape[0] // TM
    proj = _in_proj(x, mod, lw["w_in"], l, tiles_per_group)
    if latent is None:
        kc_all, vc_all, s_all = ctx_out
        o_gdn, s_all = _gdn(proj, lw["conv_w"], lw["a_row"], lw["d_row"], lw["gdn_norm_w"], None, s_all, l, bsz, t)
        o_att, kc_all, vc_all = _attn_ctx(proj, lw["q_norm_w"], lw["k_norm_w"], kc_all, vc_all, l, bsz, t)
        ctx_out = (kc_all, vc_all, s_all)
    else:
        cache_k, cache_v, state_gdn, cos_t, sin_t = latent
        (o_gdn,) = _gdn(proj, lw["conv_w"], lw["a_row"], lw["d_row"], lw["gdn_norm_w"], state_gdn, None, l, bsz, t)
        o_att = _attn_lat(proj, cache_k, cache_v, lw["q_norm_w"], lw["k_norm_w"], cos_t, sin_t, l, bsz, t, 256)
    x1 = _out_proj(o_gdn, o_att, lw["w_o"], l, x, mod, lw["ln1_g"], lw["ln1_b"], tiles_per_group)
    act = _ffn_up(x1, mod, lw["w_gate_up"], l, tiles_per_group * TM)
    x2 = _ffn_down(act, lw["w_down"], l, x1, mod, lw["ln2_g"], lw["ln2_b"], tiles_per_group)
    return x2, ctx_out


def kernel(x_prompt, x_sample, cache_k, cache_v, state_gdn, c, c_ctx, w_ada, b_ada, w_in, conv_w, a_log, dt_bias,
           gdn_norm_w, q_norm_w, k_norm_w, w_o, ln1_g, ln1_b, ln2_g, ln2_b, w_gate_up, w_down):
    bsz, seq, _ = x_prompt.shape
    dbsz, dseq, _ = x_sample.shape
    past = cache_k.shape[2]

    i_b = QKV_W + GDN_W
    i_q = i_b + 4 * H_GDN
    w_in_r = jnp.concatenate(
        [w_in[:, :, :i_b], w_in[:, :, i_q:], w_in[:, :, i_b:i_q],
         jnp.zeros((DEPTH, D_MODEL, LANE - 4 * H_GDN), w_in.dtype)], axis=-1).astype(BF16)
    w_o_b = w_o.astype(BF16)
    w_down_b = w_down.astype(BF16)

    cond8 = jnp.concatenate([c_ctx[None, :], c, jnp.zeros((8 - 1 - dbsz, D_MODEL), F32)], axis=0)
    mod = _ada(cond8, w_ada, b_ada.reshape(DEPTH, 1, 6 * D_MODEL)).reshape(DEPTH, 8, 6, D_MODEL)

    cos_t, sin_t = _rope_lane_tables(dseq)
    ck = cache_k.reshape(dbsz, DEPTH, past, KV_W)
    cv = cache_v.reshape(dbsz, DEPTH, past, KV_W)

    def gate_rows(p):
        return jnp.zeros((1, LANE), F32).at[0, 2 * H_GDN:4 * H_GDN].set(p.reshape(-1))

    xp = x_prompt.reshape(bsz * seq, D_MODEL)
    xs = x_sample.reshape(dbsz * dseq, D_MODEL)
    ctx_out = (jnp.zeros((bsz, DEPTH, seq, KV_W), F32), jnp.zeros((bsz, DEPTH, seq, KV_W), F32),
               jnp.zeros((bsz, DEPTH, 2, H_GDN, DK, DK), F32))
    for l in range(DEPTH):
        lw = {
            "w_in": w_in_r, "conv_w": conv_w[l], "a_row": gate_rows(a_log[l]), "d_row": gate_rows(dt_bias[l]),
            "gdn_norm_w": gdn_norm_w[l].reshape(1, DK), "q_norm_w": q_norm_w[l].reshape(1, HEAD_DIM),
            "k_norm_w": k_norm_w[l].reshape(1, HEAD_DIM), "w_o": w_o_b,
            "ln1_g": ln1_g[l].reshape(1, D_MODEL), "ln1_b": ln1_b[l].reshape(1, D_MODEL),
            "ln2_g": ln2_g[l].reshape(1, D_MODEL), "ln2_b": ln2_b[l].reshape(1, D_MODEL),
            "w_gate_up": w_gate_up, "w_down": w_down_b,
        }
        xp, ctx_out = _trunk_layer(xp, mod[l, 0:1], lw, l, bsz, seq, None, ctx_out)
        xs, _ = _trunk_layer(xs, mod[l, 1:1 + dbsz], lw, l, dbsz, dseq, (ck, cv, state_gdn, cos_t, sin_t), None)
    kc_all, vc_all, s_all = ctx_out
    return (xp.reshape(bsz, seq, D_MODEL), xs.reshape(dbsz, dseq, D_MODEL),
            kc_all.reshape(bsz, DEPTH, seq, N_KV, HEAD_DIM), vc_all.reshape(bsz, DEPTH, seq, N_KV, HEAD_DIM), s_all)
```

```python
import functools
import math

import jax
import jax.numpy as jnp
from jax import lax
from jax.experimental import pallas as pl
from jax.experimental.pallas import tpu as pltpu

D_MODEL = 2048
DEPTH = 2
GRID_W = 64
DK = 128
H_GDN = 8
GDN_W = H_GDN * DK
HEAD_DIM = 128
N_Q = 8
N_KV = 2
GRP = N_Q // N_KV
ATTN_W = N_Q * HEAD_DIM
KV_W = N_KV * HEAD_DIM
CONV_K = 5
D_FF = 5632
ROPE_BASE = 10000.0
RMS_EPS = 1e-6
LN_EPS = 1e-5
DN_ALPHA = (2 * DEPTH) ** 0.25
QKV_W = 3 * GDN_W
IN_W = QKV_W + GDN_W + 4 * H_GDN + ATTN_W + 2 * KV_W

LANE = 128
GDN_CHUNK = 128
GDN_HB = 2
GDN_HB_SHORT = 8
GDN_SHORT_T = 256
GDN_CG = 4
CONV_PAD = 8
COL_Z = QKV_W
COL_Q = COL_Z + GDN_W
COL_K = COL_Q + ATTN_W
COL_V = COL_K + KV_W
COL_BA = COL_V + KV_W
IN_W_PAD = COL_BA + LANE

TM = 512
TM_UP = 1024
TM_IN = 1024
TN_IN = 1920
TN_FF = 512
TK_FF = 2816
TN_ADA = 1024
VMEM_LIMIT = 56 * 1024 * 1024

F32 = jnp.float32
BF16 = jnp.bfloat16


def _cparams(sem):
    return pltpu.CompilerParams(dimension_semantics=sem, vmem_limit_bytes=VMEM_LIMIT)


def _dot(a, b):
    return jnp.dot(a, b, preferred_element_type=F32)


def _dot_nt(a, b):
    return lax.dot_general(a, b, (((1,), (1,)), ((), ())), preferred_element_type=F32)


def _dot_tn(a, b):
    return lax.dot_general(a, b, (((0,), (0,)), ((), ())), preferred_element_type=F32)


def _sigmoid(x):
    return 1.0 / (1.0 + jnp.exp(-x))


def _silu(x):
    return x * _sigmoid(x)


def _layer_norm(y, g, b):
    mu = jnp.mean(y, axis=-1, keepdims=True)
    yc = y - mu
    var = jnp.mean(yc * yc, axis=-1, keepdims=True)
    return yc * lax.rsqrt(var + LN_EPS) * g + b


def _rms_norm(x, w):
    return x * lax.rsqrt(jnp.mean(x * x, axis=-1, keepdims=True) + RMS_EPS) * w


def _ada_kernel(c_ref, w_ref, b_ref, o_ref):
    c = c_ref[...]
    o_ref[...] = _dot(_silu(c).astype(BF16), w_ref[...].astype(BF16)) + b_ref[...]


def _ada(cond8, w_ada, b_ada):
    n = w_ada.shape[-1]
    return pl.pallas_call(
        _ada_kernel,
        grid=(DEPTH, n // TN_ADA),
        in_specs=[
            pl.BlockSpec((8, D_MODEL), lambda l, j: (0, 0)),
            pl.BlockSpec((None, D_MODEL, TN_ADA), lambda l, j: (l, 0, j)),
            pl.BlockSpec((None, 1, TN_ADA), lambda l, j: (l, 0, j)),
        ],
        out_specs=pl.BlockSpec((None, 8, TN_ADA), lambda l, j: (l, 0, j)),
        out_shape=jax.ShapeDtypeStruct((DEPTH, 8, n), F32),
        compiler_params=_cparams(("arbitrary", "arbitrary")),
        name="ada",
    )(cond8, w_ada, b_ada)


def _inproj_kernel(x_ref, m_ref, w_ref, o_ref, h_ref):
    @pl.when(pl.program_id(1) == 0)
    def _():
        h_ref[...] = (x_ref[...] * (1.0 + m_ref[1:2, :]) + m_ref[0:1, :]).astype(BF16)

    o_ref[...] = _dot(h_ref[...], w_ref[...])


def _in_proj(x, mod, w, l, rows_per_group):
    n = x.shape[0]
    tm = min(TM_IN, rows_per_group)
    tiles_per_group = rows_per_group // tm
    return pl.pallas_call(
        _inproj_kernel,
        grid=(n // tm, IN_W_PAD // TN_IN),
        in_specs=[
            pl.BlockSpec((tm, D_MODEL), lambda i, j: (i, 0)),
            pl.BlockSpec((None, 6, D_MODEL), lambda i, j: (i // tiles_per_group, 0, 0)),
            pl.BlockSpec((None, D_MODEL, TN_IN), lambda i, j: (l, 0, j)),
        ],
        out_specs=pl.BlockSpec((tm, TN_IN), lambda i, j: (i, j)),
        out_shape=jax.ShapeDtypeStruct((n, IN_W_PAD), F32),
        scratch_shapes=[pltpu.VMEM((tm, D_MODEL), BF16)],
        compiler_params=_cparams(("parallel", "arbitrary")),
        name="in_proj",
    )(x, mod, w)


def _conv_silu(xp, i, t, w_ref, cols):
    acc = None
    for j in range(CONV_K):
        start = CONV_PAD - CONV_K // 2 + j
        term = xp[i, start:start + t, :] * w_ref[j:j + 1, cols]
        acc = term if acc is None else acc + term
    return _silu(acc)


def _seg_cumsum(x, reverse):
    t = x.shape[0]
    pos = lax.broadcasted_iota(jnp.int32, x.shape, 0) % GDN_CHUNK
    s = 1
    while s < GDN_CHUNK:
        if reverse:
            x = x + jnp.where(pos < GDN_CHUNK - s, pltpu.roll(x, shift=t - s, axis=0), 0.0)
        else:
            x = x + jnp.where(pos >= s, pltpu.roll(x, shift=s, axis=0), 0.0)
        s *= 2
    return x


def _unit_tri_inverses(lms):
    c = lms[0].shape[0]
    row = lax.broadcasted_iota(jnp.int32, (c, c), 0)
    col = lax.broadcasted_iota(jnp.int32, (c, c), 1)
    rc = row ^ col
    eye = (row == col).astype(F32)
    xs = None
    for lvl in range(int(math.log2(c))):
        keep = (rc >> lvl) == 1
        if xs is None:
            xs = [eye - jnp.where(keep, lm, 0.0) for lm in lms]
            lbs = [lm.astype(BF16) for lm in lms]
            zero = jnp.zeros((c, c), BF16)
            continue
        xbs = [x.astype(BF16) for x in xs]
        ys = [_dot(xb, jnp.where(keep, lb, zero)).astype(BF16) for xb, lb in zip(xbs, lbs)]
        xs = [x - _dot(y, xb) for x, y, xb in zip(xs, ys, xbs)]
    return xs


def _gdn_intra(qs, ks, vs, gs, bs, us, wb, qb, ab, kd, chunks):
    c = GDN_CHUNK
    row = lax.broadcasted_iota(jnp.int32, (c, c), 0)
    col = lax.broadcasted_iota(jnp.int32, (c, c), 1)
    incl = (row >= col, row <= col)
    strict = (row > col, row < col)
    kqs = []
    for j, r0 in chunks:
        rows = pl.ds(r0, c)
        kcb = ks[j, rows, :].astype(BF16)
        kqs.append(_dot_nt(jnp.concatenate([kcb, qs[j, rows, :].astype(BF16)], axis=0), kcb))
    work, lms = [], []
    for (j, r0), kq in zip(chunks, kqs):
        rows = pl.ds(r0, c)
        for d in range(2):
            gcum = gs[j, d, rows, :]
            beta = bs[j, d, rows, :]
            dm = gcum - gcum.T
            decay = jnp.where(incl[d], jnp.exp(jnp.where(incl[d], dm, 0.0)), 0.0)
            lms.append(jnp.where(strict[d], kq[:c] * beta * decay, 0.0))
            ab[j, d, rows, :] = (kq[c:] * decay).astype(BF16)
            work.append((j, d, rows, gcum, beta))
    xs = _unit_tri_inverses(lms)
    for (j, d, rows, gcum, beta), x in zip(work, xs):
        qc = qs[j, rows, :]
        kc = ks[j, rows, :]
        egc = jnp.exp(gcum)
        rhs = jnp.concatenate([vs[j, rows, :] * beta, kc * (beta * egc)], axis=1)
        sol = _dot(x.astype(BF16), rhs.astype(BF16))
        g_last = gcum[c - 1:c, :] if d == 0 else gcum[0:1, :]
        us[j, d, rows, :] = sol[:, :DK]
        wb[j, d, rows, :] = sol[:, DK:].astype(BF16)
        qb[j, d, rows, :] = (qc * egc).astype(BF16)
        kd[j, d, rows, :] = (kc * jnp.exp(g_last - gcum)).astype(BF16)


def _gdn_scan_step(gs, us, wb, qb, ab, kd, ss, os, j, d, r0, zero_state):
    c = GDN_CHUNK
    rows = pl.ds(r0, c)
    if zero_state:
        v_new = us[j, d, rows, :].astype(BF16)
        os[j, rows, :] += _dot(ab[j, d, rows, :], v_new)
        ss[j, d] = _dot_tn(kd[j, d, rows, :], v_new)
        return
    s_state = ss[j, d]
    ws_qs = _dot(jnp.concatenate([wb[j, d, rows, :], qb[j, d, rows, :]], axis=0), s_state.astype(BF16))
    v_new = (us[j, d, rows, :] - ws_qs[:c]).astype(BF16)
    os[j, rows, :] += ws_qs[c:] + _dot(ab[j, d, rows, :], v_new)
    g_last = gs[j, d, pl.ds(r0 + (c - 1 if d == 0 else 0), 1), :]
    ss[j, d] = s_state * jnp.exp(g_last) + _dot_tn(kd[j, d, rows, :], v_new)


def _gdn_kernel(q_ref, k_ref, v_ref, z_ref, ba_ref, arow_ref, drow_ref, cwq_ref, cwk_ref, cwv_ref, nw_ref, *rest,
                has_s0, has_sout):
    rest = list(rest)
    s0_ref = rest.pop(0) if has_s0 else None
    if has_sout:
        rest.pop(0)
    o_ref = rest.pop(0)
    sout_ref = rest.pop(0) if has_sout else None
    xp, qs, ks, vs, gate, gs, bs, us, wb, qb, ab, kd, ss, os = rest
    hp = pl.program_id(1)
    t = q_ref.shape[0]
    hb = q_ref.shape[1] // DK
    lane = lax.broadcasted_iota(jnp.int32, (t, LANE), 1)

    @pl.when(hp == 0)
    def _():
        ba = ba_ref[...]
        x = ba + drow_ref[...]
        softplus = jnp.maximum(x, 0.0) + jnp.log(1.0 + jnp.exp(-jnp.abs(x)))
        g = -jnp.exp(arow_ref[...]) * softplus
        gcum = jnp.where(lane >= 3 * H_GDN, _seg_cumsum(g, True), _seg_cumsum(g, False))
        gate[...] = jnp.where(lane < 2 * H_GDN, _sigmoid(ba), gcum)

    pad0 = jnp.zeros((CONV_PAD, DK), F32)
    for i, ref in enumerate((q_ref, k_ref, v_ref)):
        for j in range(hb):
            xp[i * hb + j, 0:CONV_PAD, :] = pad0
            xp[i * hb + j, CONV_PAD + t:, :] = pad0
            xp[i * hb + j, CONV_PAD:CONV_PAD + t, :] = ref[:, j * DK:(j + 1) * DK]

    for j in range(hb):
        h = hp * hb + j
        sl = slice(j * DK, (j + 1) * DK)
        q = _conv_silu(xp, j, t, cwq_ref, sl)
        qs[j] = q * lax.rsqrt(jnp.sum(q * q, axis=-1, keepdims=True) + RMS_EPS) * (DK ** -0.5)
        k = _conv_silu(xp, hb + j, t, cwk_ref, sl)
        ks[j] = k * lax.rsqrt(jnp.sum(k * k, axis=-1, keepdims=True) + RMS_EPS)
        vs[j] = _conv_silu(xp, 2 * hb + j, t, cwv_ref, sl)
        gt = gate[...]
        for d in range(2):
            bcol = jnp.sum(jnp.where(lane == d * H_GDN + h, gt, 0.0), axis=-1, keepdims=True)
            gcol = jnp.sum(jnp.where(lane == (2 + d) * H_GDN + h, gt, 0.0), axis=-1, keepdims=True)
            bs[j, d] = jnp.broadcast_to(bcol, (t, LANE))
            gs[j, d] = jnp.broadcast_to(gcol, (t, LANE))
            if has_s0:
                ss[j, d] = s0_ref[d, j]
        os[j] = jnp.zeros((t, DK), F32)

    n_chunks = t // GDN_CHUNK
    cg = min(GDN_CG, n_chunks)
    n_groups = n_chunks // cg

    def intra_group(i):
        chunks = []
        for j in range(hb):
            for cc in range(cg):
                r0 = (i * cg + cc) * GDN_CHUNK
                if not isinstance(r0, int):
                    r0 = pl.multiple_of(r0, GDN_CHUNK)
                chunks.append((j, r0))
        _gdn_intra(qs, ks, vs, gs, bs, us, wb, qb, ab, kd, chunks)

    def scan_step(i, first=False):
        for j in range(hb):
            for d in range(2):
                r0 = (i if d == 0 else n_chunks - 1 - i) * GDN_CHUNK
                if not isinstance(r0, int):
                    r0 = pl.multiple_of(r0, GDN_CHUNK)
                _gdn_scan_step(gs, us, wb, qb, ab, kd, ss, os, j, d, r0, zero_state=first and not has_s0)

    if n_groups == 1:
        intra_group(0)
    else:
        lax.fori_loop(0, n_groups, lambda i, c: (intra_group(i), c)[1], 0)
    scan_step(0, first=True)
    if n_chunks <= 2:
        for i in range(1, n_chunks):
            scan_step(i)
    else:
        lax.fori_loop(1, n_chunks, lambda i, c: (scan_step(i), c)[1], 0)

    for j in range(hb):
        sl = slice(j * DK, (j + 1) * DK)
        o_ref[:, sl] = (_rms_norm(os[j], nw_ref[...]) * _silu(z_ref[:, sl])).astype(o_ref.dtype)
        if has_sout:
            for d in range(2):
                sout_ref[d, j] = ss[j, d]


def _gdn(proj, conv_w, a_row, d_row, norm_w, s0, s_all, l, bsz, t):
    has_s0 = s0 is not None
    want_state = s_all is not None
    hb = GDN_HB_SHORT if t <= GDN_SHORT_T else GDN_HB
    wblk = hb * DK
    col = lambda off: (lambda b, hp: (b, off // wblk + hp))
    row0 = lambda b, hp: (0, 0)
    in_specs = [
        pl.BlockSpec((t, wblk), col(0)),
        pl.BlockSpec((t, wblk), col(GDN_W)),
        pl.BlockSpec((t, wblk), col(2 * GDN_W)),
        pl.BlockSpec((t, wblk), col(COL_Z)),
        pl.BlockSpec((t, LANE), lambda b, hp: (b, COL_BA // LANE)),
        pl.BlockSpec((1, LANE), row0),
        pl.BlockSpec((1, LANE), row0),
        pl.BlockSpec((CONV_K, wblk), lambda b, hp: (0, hp)),
        pl.BlockSpec((CONV_K, wblk), lambda b, hp: (0, GDN_W // wblk + hp)),
        pl.BlockSpec((CONV_K, wblk), lambda b, hp: (0, 2 * GDN_W // wblk + hp)),
        pl.BlockSpec((1, DK), row0),
    ]
    args = [proj, proj, proj, proj, proj, a_row, d_row, conv_w, conv_w, conv_w, norm_w]
    if has_s0:
        in_specs.append(pl.BlockSpec((None, None, 2, hb, DK, DK), lambda b, hp: (b, l, 0, hp, 0, 0)))
        args.append(s0)
    out_specs = [pl.BlockSpec((t, wblk), lambda b, hp: (b, hp))]
    out_shape = [jax.ShapeDtypeStruct((bsz * t, GDN_W), BF16)]
    aliases = {}
    if want_state:
        aliases = {len(args): 1}
        in_specs.append(pl.BlockSpec(memory_space=pl.ANY))
        args.append(s_all)
        out_specs.append(pl.BlockSpec((None, None, 2, hb, DK, DK), lambda b, hp: (b, l, 0, hp, 0, 0)))
        out_shape.append(jax.ShapeDtypeStruct(s_all.shape, F32))
    per_head = lambda dt: pltpu.VMEM((hb, t, DK), dt)
    per_dir = lambda dt: pltpu.VMEM((hb, 2, t, DK), dt)
    return pl.pallas_call(
        functools.partial(_gdn_kernel, has_s0=has_s0, has_sout=want_state),
        grid=(bsz, H_GDN // hb),
        in_specs=in_specs,
        out_specs=out_specs,
        out_shape=out_shape,
        input_output_aliases=aliases,
        scratch_shapes=[
            pltpu.VMEM((3 * hb, t + 2 * CONV_PAD, DK), F32),
            per_head(F32), per_head(F32), per_head(F32),
            pltpu.VMEM((t, LANE), F32),
            per_dir(F32), per_dir(F32),
            per_dir(F32), per_dir(BF16), per_dir(BF16), per_dir(BF16), per_dir(BF16),
            pltpu.VMEM((hb, 2, DK, DK), F32),
            per_head(F32),
        ],
        compiler_params=_cparams(("parallel", "arbitrary")),
        name="gdn_lat" if has_s0 else "gdn_ctx",
    )(*args)


def _rope(x, cos, sin):
    lane = lax.broadcasted_iota(jnp.int32, x.shape, 1)
    quarter = HEAD_DIM // 4
    partner = jnp.where((lane % (2 * quarter)) < quarter,
                        pltpu.roll(x, shift=HEAD_DIM - quarter, axis=1),
                        pltpu.roll(x, shift=quarter, axis=1))
    return x * cos + partner * sin


def _softmax_pv(score_blocks, value_blocks):
    m = None
    for s in score_blocks:
        bm = jnp.max(s, axis=-1, keepdims=True)
        m = bm if m is None else jnp.maximum(m, bm)
    den, acc = None, None
    for s, v in zip(score_blocks, value_blocks):
        p = jnp.exp(s - m)
        ps = jnp.sum(p, axis=-1, keepdims=True)
        pv = _dot(p.astype(BF16), v)
        den = ps if den is None else den + ps
        acc = pv if acc is None else acc + pv
    return acc / den


def _attn_ctx_kernel(q_ref, k_ref, v_ref, qw_ref, kw_ref, kc_all_ref, vc_all_ref, o_ref, kc_ref, vc_ref):
    del kc_all_ref, vc_all_ref
    scale = HEAD_DIM ** -0.5
    vc_ref[...] = v_ref[...]
    for g in range(N_KV):
        gl = slice(g * HEAD_DIM, (g + 1) * HEAD_DIM)
        kg = _rms_norm(k_ref[:, gl], kw_ref[...])
        kc_ref[:, gl] = kg
        kgb = kg.astype(BF16)
        vgb = v_ref[:, gl].astype(BF16)
        for hh in range(GRP):
            hl = slice((g * GRP + hh) * HEAD_DIM, (g * GRP + hh + 1) * HEAD_DIM)
            qh = _rms_norm(q_ref[:, hl], qw_ref[...]).astype(BF16)
            s = _dot_nt(qh, kgb) * scale
            o_ref[:, hl] = _softmax_pv([s], [vgb]).astype(o_ref.dtype)


def _attn_ctx(proj, q_norm_w, k_norm_w, kc_all, vc_all, l, bsz, t):
    cache_spec = pl.BlockSpec((None, None, t, KV_W), lambda b: (b, l, 0, 0))
    return pl.pallas_call(
        _attn_ctx_kernel,
        grid=(bsz,),
        in_specs=[
            pl.BlockSpec((t, ATTN_W), lambda b: (b, COL_Q // ATTN_W)),
            pl.BlockSpec((t, KV_W), lambda b: (b, COL_K // KV_W)),
            pl.BlockSpec((t, KV_W), lambda b: (b, COL_V // KV_W)),
            pl.BlockSpec((1, HEAD_DIM), lambda b: (0, 0)),
            pl.BlockSpec((1, HEAD_DIM), lambda b: (0, 0)),
            pl.BlockSpec(memory_space=pl.ANY),
            pl.BlockSpec(memory_space=pl.ANY),
        ],
        out_specs=[pl.BlockSpec((t, ATTN_W), lambda b: (b, 0)), cache_spec, cache_spec],
        out_shape=[
            jax.ShapeDtypeStruct((bsz * t, ATTN_W), BF16),
            jax.ShapeDtypeStruct(kc_all.shape, F32),
            jax.ShapeDtypeStruct(vc_all.shape, F32),
        ],
        input_output_aliases={5: 1, 6: 2},
        compiler_params=_cparams(("parallel",)),
        name="attn_ctx",
    )(proj, proj, proj, q_norm_w, k_norm_w, kc_all, vc_all)


def _attn_lat_kernel(q_ref, k_ref, v_ref, ck_ref, cv_ref, qw_ref, kw_ref, cq_ref, sq_ref, ck_cos_ref, ck_sin_ref,
                     o_ref):
    scale = HEAD_DIM ** -0.5
    for g in range(N_KV):
        gl = slice(g * HEAD_DIM, (g + 1) * HEAD_DIM)
        kg = _rope(_rms_norm(k_ref[:, gl], kw_ref[...]), ck_cos_ref[...], ck_sin_ref[...]).astype(BF16)
        vg = v_ref[:, gl].astype(BF16)
        ckg = ck_ref[:, gl].astype(BF16)
        cvg = cv_ref[:, gl].astype(BF16)
        for hh in range(GRP):
            hl = slice((g * GRP + hh) * HEAD_DIM, (g * GRP + hh + 1) * HEAD_DIM)
            qh = _rope(_rms_norm(q_ref[:, hl], qw_ref[...]), cq_ref[...], sq_ref[...]).astype(BF16)
            s1 = _dot_nt(qh, kg) * scale
            s2 = _dot_nt(qh, ckg) * scale
            o_ref[:, hl] = _softmax_pv([s1, s2], [vg, cvg]).astype(o_ref.dtype)


def _attn_lat(proj, cache_k, cache_v, q_norm_w, k_norm_w, cos_t, sin_t, l, bsz, t, tq):
    past = cache_k.shape[2]
    nq = t // tq
    return pl.pallas_call(
        _attn_lat_kernel,
        grid=(bsz, nq),
        in_specs=[
            pl.BlockSpec((tq, ATTN_W), lambda b, i: (b * nq + i, COL_Q // ATTN_W)),
            pl.BlockSpec((t, KV_W), lambda b, i: (b, COL_K // KV_W)),
            pl.BlockSpec((t, KV_W), lambda b, i: (b, COL_V // KV_W)),
            pl.BlockSpec((None, None, past, KV_W), lambda b, i: (b, l, 0, 0)),
            pl.BlockSpec((None, None, past, KV_W), lambda b, i: (b, l, 0, 0)),
            pl.BlockSpec((1, HEAD_DIM), lambda b, i: (0, 0)),
            pl.BlockSpec((1, HEAD_DIM), lambda b, i: (0, 0)),
            pl.BlockSpec((tq, HEAD_DIM), lambda b, i: (i, 0)),
            pl.BlockSpec((tq, HEAD_DIM), lambda b, i: (i, 0)),
            pl.BlockSpec((t, HEAD_DIM), lambda b, i: (0, 0)),
            pl.BlockSpec((t, HEAD_DIM), lambda b, i: (0, 0)),
        ],
        out_specs=pl.BlockSpec((tq, ATTN_W), lambda b, i: (b * nq + i, 0)),
        out_shape=jax.ShapeDtypeStruct((bsz * t, ATTN_W), BF16),
        compiler_params=_cparams(("parallel", "arbitrary")),
        name="attn_lat",
    )(proj, proj, proj, cache_k, cache_v, q_norm_w, k_norm_w, cos_t, sin_t, cos_t, sin_t)


def _rope_lane_tables(n_tok):
    rows = n_tok // GRID_W
    row = jnp.repeat(jnp.arange(rows), GRID_W).astype(F32)
    colp = jnp.tile(jnp.arange(GRID_W), rows).astype(F32)
    half = HEAD_DIM // 2
    inv_freq = ROPE_BASE ** (-jnp.arange(0, half, 2, dtype=F32) / half)
    ang_r = row[:, None] * inv_freq
    ang_c = colp[:, None] * inv_freq
    cos_t = jnp.concatenate([jnp.cos(ang_r), jnp.cos(ang_r), jnp.cos(ang_c), jnp.cos(ang_c)], axis=-1)
    sin_t = jnp.concatenate([-jnp.sin(ang_r), jnp.sin(ang_r), -jnp.sin(ang_c), jnp.sin(ang_c)], axis=-1)
    return cos_t, sin_t


def _outproj_kernel(og_ref, oa_ref, wg_ref, wa_ref, x_ref, m_ref, g_ref, b_ref, o_ref):
    mix = _dot(og_ref[...], wg_ref[...]) + _dot(oa_ref[...], wa_ref[...])
    y = DN_ALPHA * x_ref[...] + m_ref[2:3, :] * mix
    o_ref[...] = _layer_norm(y, g_ref[...], b_ref[...])


def _out_proj(o_gdn, o_att, w_o, l, x, mod, ln_g, ln_b, tiles_per_group):
    n = x.shape[0]
    row = lambda i: (i, 0)
    const = lambda i: (0, 0)
    return pl.pallas_call(
        _outproj_kernel,
        grid=(n // TM,),
        in_specs=[
            pl.BlockSpec((TM, GDN_W), row),
            pl.BlockSpec((TM, ATTN_W), row),
            pl.BlockSpec((None, GDN_W, D_MODEL), lambda i: (l, 0, 0)),
            pl.BlockSpec((None, ATTN_W, D_MODEL), lambda i: (l, 1, 0)),
            pl.BlockSpec((TM, D_MODEL), row),
            pl.BlockSpec((None, 6, D_MODEL), lambda i: (i // tiles_per_group, 0, 0)),
            pl.BlockSpec((1, D_MODEL), const),
            pl.BlockSpec((1, D_MODEL), const),
        ],
        out_specs=pl.BlockSpec((TM, D_MODEL), row),
        out_shape=jax.ShapeDtypeStruct((n, D_MODEL), F32),
        compiler_params=_cparams(("parallel",)),
        name="out_proj",
    )(o_gdn, o_att, w_o, w_o, x, mod, ln_g, ln_b)


def _ffn_up_kernel(x_ref, m_ref, wg_ref, wu_ref, o_ref, h_ref):
    @pl.when(pl.program_id(1) == 0)
    def _():
        h_ref[...] = (x_ref[...] * (1.0 + m_ref[4:5, :]) + m_ref[3:4, :]).astype(BF16)

    h = h_ref[...]
    gate = _dot(h, wg_ref[...].astype(BF16))
    up = _dot(h, wu_ref[...].astype(BF16))
    o_ref[...] = (_silu(gate) * up).astype(o_ref.dtype)


def _ffn_up(x, mod, w_gu, l, rows_per_group):
    n = x.shape[0]
    nj = D_FF // TN_FF
    tm = min(TM_UP, rows_per_group)
    tiles_per_group = rows_per_group // tm
    return pl.pallas_call(
        _ffn_up_kernel,
        grid=(n // tm, nj),
        in_specs=[
            pl.BlockSpec((tm, D_MODEL), lambda i, j: (i, 0)),
            pl.BlockSpec((None, 6, D_MODEL), lambda i, j: (i // tiles_per_group, 0, 0)),
            pl.BlockSpec((None, D_MODEL, TN_FF), lambda i, j: (l, 0, j)),
            pl.BlockSpec((None, D_MODEL, TN_FF), lambda i, j: (l, 0, nj + j)),
        ],
        out_specs=pl.BlockSpec((tm, TN_FF), lambda i, j: (i, j)),
        out_shape=jax.ShapeDtypeStruct((n, D_FF), BF16),
        scratch_shapes=[pltpu.VMEM((tm, D_MODEL), BF16)],
        compiler_params=_cparams(("parallel", "arbitrary")),
        name="ffn_up",
    )(x, mod, w_gu, w_gu)


def _ffn_down_kernel(a_ref, w_ref, x_ref, m_ref, g_ref, b_ref, o_ref, acc_ref):
    k = pl.program_id(1)

    part = _dot(a_ref[...], w_ref[...])

    last = pl.num_programs(1) - 1

    @pl.when(k == 0)
    def _():
        acc_ref[...] = part

    @pl.when((k > 0) & (k < last))
    def _():
        acc_ref[...] += part

    @pl.when(k == last)
    def _():
        y = DN_ALPHA * x_ref[...] + m_ref[5:6, :] * (acc_ref[...] + part)
        o_ref[...] = _layer_norm(y, g_ref[...], b_ref[...])


def _ffn_down(act, w_down, l, x, mod, ln_g, ln_b, tiles_per_group):
    n = x.shape[0]
    return pl.pallas_call(
        _ffn_down_kernel,
        grid=(n // TM, D_FF // TK_FF),
        in_specs=[
            pl.BlockSpec((TM, TK_FF), lambda i, k: (i, k)),
            pl.BlockSpec((None, TK_FF, D_MODEL), lambda i, k: (l, k, 0)),
            pl.BlockSpec((TM, D_MODEL), lambda i, k: (i, 0)),
            pl.BlockSpec((None, 6, D_MODEL), lambda i, k: (i // tiles_per_group, 0, 0)),
            pl.BlockSpec((1, D_MODEL), lambda i, k: (0, 0)),
            pl.BlockSpec((1, D_MODEL), lambda i, k: (0, 0)),
        ],
        out_specs=pl.BlockSpec((TM, D_MODEL), lambda i, k: (i, 0)),
        out_shape=jax.ShapeDtypeStruct((n, D_MODEL), F32),
        scratch_shapes=[pltpu.VMEM((TM, D_MODEL), F32)],
        compiler_params=_cparams(("parallel", "arbitrary")),
        name="ffn_down",
    )(act, w_down, x, mod, ln_g, ln_b)


def _trunk_layer(x, mod, lw, l, bsz, t, latent, ctx_out):
    assert t % TM == 0 or TM % t == 0
    tiles_per_group = max(t // TM, 1) if latent is not None else x.shape[0] // TM
    proj = _in_proj(x, mod, lw["w_in"], l, tiles_per_group * TM)
    if latent is None:
        kc_all, vc_all, s_all = ctx_out
        o_gdn, s_all = _gdn(proj, lw["conv_w"], lw["a_row"], lw["d_row"], lw["gdn_norm_w"], None, s_all, l, bsz, t)
        o_att, kc_all, vc_all = _attn_ctx(proj, lw["q_norm_w"], lw["k_norm_w"], kc_all, vc_all, l, bsz, t)
        ctx_out = (kc_all, vc_all, s_all)
    else:
        cache_k, cache_v, state_gdn, cos_t, sin_t = latent
        (o_gdn,) = _gdn(proj, lw["conv_w"], lw["a_row"], lw["d_row"], lw["gdn_norm_w"], state_gdn, None, l, bsz, t)
        o_att = _attn_lat(proj, cache_k, cache_v, lw["q_norm_w"], lw["k_norm_w"], cos_t, sin_t, l, bsz, t, 256)
    x1 = _out_proj(o_gdn, o_att, lw["w_o"], l, x, mod, lw["ln1_g"], lw["ln1_b"], tiles_per_group)
    act = _ffn_up(x1, mod, lw["w_gate_up"], l, tiles_per_group * TM)
    x2 = _ffn_down(act, lw["w_down"], l, x1, mod, lw["ln2_g"], lw["ln2_b"], tiles_per_group)
    return x2, ctx_out


def kernel(x_prompt, x_sample, cache_k, cache_v, state_gdn, c, c_ctx, w_ada, b_ada, w_in, conv_w, a_log, dt_bias,
           gdn_norm_w, q_norm_w, k_norm_w, w_o, ln1_g, ln1_b, ln2_g, ln2_b, w_gate_up, w_down):
    bsz, seq, _ = x_prompt.shape
    dbsz, dseq, _ = x_sample.shape
    past = cache_k.shape[2]

    i_b = QKV_W + GDN_W
    i_q = i_b + 4 * H_GDN
    w_in_r = jnp.concatenate(
        [w_in[:, :, :i_b], w_in[:, :, i_q:], w_in[:, :, i_b:i_q],
         jnp.zeros((DEPTH, D_MODEL, LANE - 4 * H_GDN), w_in.dtype)], axis=-1).astype(BF16)
    w_o_b = w_o.astype(BF16)
    w_down_b = w_down.astype(BF16)

    cond8 = jnp.concatenate([c_ctx[None, :], c, jnp.zeros((8 - 1 - dbsz, D_MODEL), F32)], axis=0)
    mod = _ada(cond8, w_ada, b_ada.reshape(DEPTH, 1, 6 * D_MODEL)).reshape(DEPTH, 8, 6, D_MODEL)

    cos_t, sin_t = _rope_lane_tables(dseq)
    ck = cache_k.reshape(dbsz, DEPTH, past, KV_W)
    cv = cache_v.reshape(dbsz, DEPTH, past, KV_W)

    def gate_rows(p):
        return jnp.zeros((1, LANE), F32).at[0, 2 * H_GDN:4 * H_GDN].set(p.reshape(-1))

    xp = x_prompt.reshape(bsz * seq, D_MODEL)
    xs = x_sample.reshape(dbsz * dseq, D_MODEL)
    ctx_out = (jnp.zeros((bsz, DEPTH, seq, KV_W), F32), jnp.zeros((bsz, DEPTH, seq, KV_W), F32),
               jnp.zeros((bsz, DEPTH, 2, H_GDN, DK, DK), F32))
    for l in range(DEPTH):
        lw = {
            "w_in": w_in_r, "conv_w": conv_w[l], "a_row": gate_rows(a_log[l]), "d_row": gate_rows(dt_bias[l]),
            "gdn_norm_w": gdn_norm_w[l].reshape(1, DK), "q_norm_w": q_norm_w[l].reshape(1, HEAD_DIM),
            "k_norm_w": k_norm_w[l].reshape(1, HEAD_DIM), "w_o": w_o_b,
            "ln1_g": ln1_g[l].reshape(1, D_MODEL), "ln1_b": ln1_b[l].reshape(1, D_MODEL),
            "ln2_g": ln2_g[l].reshape(1, D_MODEL), "ln2_b": ln2_b[l].reshape(1, D_MODEL),
            "w_gate_up": w_gate_up, "w_down": w_down_b,
        }
        xp, ctx_out = _trunk_layer(xp, mod[l, 0:1], lw, l, bsz, seq, None, ctx_out)
        xs, _ = _trunk_layer(xs, mod[l, 1:1 + dbsz], lw, l, dbsz, dseq, (ck, cv, state_gdn, cos_t, sin_t), None)
    kc_all, vc_all, s_all = ctx_out
    return (xp.reshape(bsz, seq, D_MODEL), xs.reshape(dbsz, dseq, D_MODEL),
            kc_all.reshape(bsz, DEPTH, seq, N_KV, HEAD_DIM), vc_all.reshape(bsz, DEPTH, seq, N_KV, HEAD_DIM), s_all)
```

```python
import functools
import math

import jax
import jax.numpy as jnp
from jax import lax
from jax.experimental import pallas as pl
from jax.experimental.pallas import tpu as pltpu

D_MODEL = 2048
DEPTH = 2
GRID_W = 64
DK = 128
H_GDN = 8
GDN_W = H_GDN * DK
HEAD_DIM = 128
N_Q = 8
N_KV = 2
GRP = N_Q // N_KV
ATTN_W = N_Q * HEAD_DIM
KV_W = N_KV * HEAD_DIM
CONV_K = 5
D_FF = 5632
ROPE_BASE = 10000.0
RMS_EPS = 1e-6
LN_EPS = 1e-5
DN_ALPHA = (2 * DEPTH) ** 0.25
QKV_W = 3 * GDN_W
IN_W = QKV_W + GDN_W + 4 * H_GDN + ATTN_W + 2 * KV_W

LANE = 128
GDN_CHUNK = 128
GDN_HB = 2
GDN_HB_SHORT = 8
GDN_SHORT_T = 256
GDN_CG = 4
CONV_PAD = 8
GDN_STATIC_CHUNKS = 2
GDN_GROUP_HEADS = 4
GDN_SKEW = 12
COL_Z = QKV_W
COL_Q = COL_Z + GDN_W
COL_K = COL_Q + ATTN_W
COL_V = COL_K + KV_W
COL_BA = COL_V + KV_W
IN_W_PAD = COL_BA + LANE

TM = 512
TM_UP = 1024
TM_IN = 1024
TN_IN = 1920
TN_FF = 512
TK_FF = 2816
TN_ADA = 1024
VMEM_LIMIT = 56 * 1024 * 1024

F32 = jnp.float32
BF16 = jnp.bfloat16


def _cparams(sem):
    return pltpu.CompilerParams(dimension_semantics=sem, vmem_limit_bytes=VMEM_LIMIT)


def _dot(a, b):
    return jnp.dot(a, b, preferred_element_type=F32)


def _dot_nt(a, b):
    return lax.dot_general(a, b, (((1,), (1,)), ((), ())), preferred_element_type=F32)


def _dot_tn(a, b):
    return lax.dot_general(a, b, (((0,), (0,)), ((), ())), preferred_element_type=F32)


def _sigmoid(x):
    return 1.0 / (1.0 + jnp.exp(-x))


def _silu(x):
    return x * _sigmoid(x)


def _layer_norm(y, g, b):
    mu = jnp.mean(y, axis=-1, keepdims=True)
    yc = y - mu
    var = jnp.mean(yc * yc, axis=-1, keepdims=True)
    return yc * lax.rsqrt(var + LN_EPS) * g + b


def _rms_norm(x, w):
    return x * lax.rsqrt(jnp.mean(x * x, axis=-1, keepdims=True) + RMS_EPS) * w


def _ada_kernel(c_ref, w_ref, b_ref, o_ref):
    c = c_ref[...]
    o_ref[...] = _dot(_silu(c).astype(BF16), w_ref[...].astype(BF16)) + b_ref[...]


def _ada(cond8, w_ada, b_ada):
    n = w_ada.shape[-1]
    return pl.pallas_call(
        _ada_kernel,
        grid=(DEPTH, n // TN_ADA),
        in_specs=[
            pl.BlockSpec((8, D_MODEL), lambda l, j: (0, 0)),
            pl.BlockSpec((None, D_MODEL, TN_ADA), lambda l, j: (l, 0, j)),
            pl.BlockSpec((None, 1, TN_ADA), lambda l, j: (l, 0, j)),
        ],
        out_specs=pl.BlockSpec((None, 8, TN_ADA), lambda l, j: (l, 0, j)),
        out_shape=jax.ShapeDtypeStruct((DEPTH, 8, n), F32),
        compiler_params=_cparams(("arbitrary", "arbitrary")),
        name="ada",
    )(cond8, w_ada, b_ada)


def _inproj_kernel(x_ref, m_ref, w_ref, o_ref, h_ref):
    @pl.when(pl.program_id(1) == 0)
    def _():
        h_ref[...] = (x_ref[...] * (1.0 + m_ref[1:2, :]) + m_ref[0:1, :]).astype(BF16)

    o_ref[...] = _dot(h_ref[...], w_ref[...])


def _in_proj(x, mod, w, l, rows_per_group):
    n = x.shape[0]
    tm = min(TM_IN, rows_per_group)
    tiles_per_group = rows_per_group // tm
    return pl.pallas_call(
        _inproj_kernel,
        grid=(n // tm, IN_W_PAD // TN_IN),
        in_specs=[
            pl.BlockSpec((tm, D_MODEL), lambda i, j: (i, 0)),
            pl.BlockSpec((None, 6, D_MODEL), lambda i, j: (i // tiles_per_group, 0, 0)),
            pl.BlockSpec((None, D_MODEL, TN_IN), lambda i, j: (l, 0, j)),
        ],
        out_specs=pl.BlockSpec((tm, TN_IN), lambda i, j: (i, j)),
        out_shape=jax.ShapeDtypeStruct((n, IN_W_PAD), F32),
        scratch_shapes=[pltpu.VMEM((tm, D_MODEL), BF16)],
        compiler_params=_cparams(("parallel", "arbitrary")),
        name="in_proj",
    )(x, mod, w)


def _conv_silu(xp, i, t, w_ref, cols):
    acc = None
    for j in range(CONV_K):
        start = CONV_PAD - CONV_K // 2 + j
        term = xp[i, start:start + t, :] * w_ref[j:j + 1, cols]
        acc = term if acc is None else acc + term
    return _silu(acc)


def _seg_cumsum(x, reverse):
    t = x.shape[0]
    pos = lax.broadcasted_iota(jnp.int32, x.shape, 0) % GDN_CHUNK
    s = 1
    while s < GDN_CHUNK:
        if reverse:
            x = x + jnp.where(pos < GDN_CHUNK - s, pltpu.roll(x, shift=t - s, axis=0), 0.0)
        else:
            x = x + jnp.where(pos >= s, pltpu.roll(x, shift=s, axis=0), 0.0)
        s *= 2
    return x


def _unit_tri_inverses(lms, out):
    c = lms[0].shape[0]
    row = lax.broadcasted_iota(jnp.int32, (c, c), 0)
    col = lax.broadcasted_iota(jnp.int32, (c, c), 1)
    rc = row ^ col
    eye = (row == col).astype(F32)
    zero = jnp.zeros((c, c), BF16)
    xs = [eye - jnp.where(rc == 1, lm, 0.0) for lm in lms]
    lbs = [lm.astype(BF16) for lm in lms]
    for lvl in range(1, int(math.log2(c))):
        keep = (rc >> lvl) == 1
        xbs = [x.astype(BF16) for x in xs]
        ys = [_dot(xb, jnp.where(keep, lb, zero)).astype(BF16) for xb, lb in zip(xbs, lbs)]
        yield
        xs = [x - _dot(y, xb) for x, y, xb in zip(xs, ys, xbs)]
        yield
    out.extend(xs)


def _gdn_intra_stages(qs, ks, vs, gs, bs, us, wb, qb, ab, kd, chunks):
    c = GDN_CHUNK
    row = lax.broadcasted_iota(jnp.int32, (c, c), 0)
    col = lax.broadcasted_iota(jnp.int32, (c, c), 1)
    incl = (row >= col, row <= col)
    strict = (row > col, row < col)
    kqs = []
    for j, r0 in chunks:
        rows = pl.ds(r0, c)
        kcb = ks[j, rows, :].astype(BF16)
        kqs.append(_dot_nt(jnp.concatenate([kcb, qs[j, rows, :].astype(BF16)], axis=0), kcb))
    yield
    work, lms = [], []
    for (j, r0), kq in zip(chunks, kqs):
        rows = pl.ds(r0, c)
        for d in range(2):
            gcum = gs[j, d, rows, :]
            beta = bs[j, d, rows, :]
            dm = gcum - gcum.T
            decay = jnp.where(incl[d], jnp.exp(jnp.where(incl[d], dm, 0.0)), 0.0)
            lms.append(jnp.where(strict[d], kq[:c] * beta * decay, 0.0))
            ab[j, d, rows, :] = (kq[c:] * decay).astype(BF16)
            work.append((j, d, rows, gcum, beta))
    xs = []
    yield from _unit_tri_inverses(lms, xs)
    for (j, d, rows, gcum, beta), x in zip(work, xs):
        qc = qs[j, rows, :]
        kc = ks[j, rows, :]
        egc = jnp.exp(gcum)
        rhs = jnp.concatenate([vs[j, rows, :] * beta, kc * (beta * egc)], axis=1)
        sol = _dot(x.astype(BF16), rhs.astype(BF16))
        g_last = gcum[c - 1:c, :] if d == 0 else gcum[0:1, :]
        us[j, d, rows, :] = sol[:, :DK]
        wb[j, d, rows, :] = sol[:, DK:].astype(BF16)
        qb[j, d, rows, :] = (qc * egc).astype(BF16)
        kd[j, d, rows, :] = (kc * jnp.exp(g_last - gcum)).astype(BF16)
    yield


def _gdn_scan_stages(gs, us, wb, qb, ab, kd, ss, os, items, zero_state):
    c = GDN_CHUNK
    mids = []
    for j, d, r0 in items:
        rows = pl.ds(r0, c)
        if zero_state:
            mids.append((us[j, d, rows, :].astype(BF16), None, None))
        else:
            s_state = ss[j, d]
            ws_qs = _dot(jnp.concatenate([wb[j, d, rows, :], qb[j, d, rows, :]], axis=0), s_state.astype(BF16))
            mids.append(((us[j, d, rows, :] - ws_qs[:c]).astype(BF16), ws_qs[c:], s_state))
    yield
    for (j, d, r0), (v_new, q_s, s_state) in zip(items, mids):
        rows = pl.ds(r0, c)
        o = _dot(ab[j, d, rows, :], v_new)
        os[j, rows, :] += o if zero_state else q_s + o
        upd = _dot_tn(kd[j, d, rows, :], v_new)
        if zero_state:
            ss[j, d] = upd
        else:
            g_last = gs[j, d, pl.ds(r0 + (c - 1 if d == 0 else 0), 1), :]
            ss[j, d] = s_state * jnp.exp(g_last) + upd
    yield


def _run(gen):
    for _ in gen:
        pass


def _run_skewed(gens, skew):
    live, started, rnd = [], 0, 0
    while started < len(gens) or live:
        if started < len(gens) and rnd >= started * skew:
            live.append(gens[started])
            started += 1
        for g in list(live):
            try:
                next(g)
            except StopIteration:
                live.remove(g)
        rnd += 1


def _gdn_kernel(q_ref, k_ref, v_ref, z_ref, ba_ref, arow_ref, drow_ref, cwq_ref, cwk_ref, cwv_ref, nw_ref, *rest,
                has_s0, has_sout):
    rest = list(rest)
    s0_ref = rest.pop(0) if has_s0 else None
    if has_sout:
        rest.pop(0)
    o_ref = rest.pop(0)
    sout_ref = rest.pop(0) if has_sout else None
    xp, qs, ks, vs, gate, gs, bs, us, wb, qb, ab, kd, ss, os = rest
    hp = pl.program_id(1)
    t = q_ref.shape[0]
    hb = q_ref.shape[1] // DK
    lane = lax.broadcasted_iota(jnp.int32, (t, LANE), 1)

    @pl.when(hp == 0)
    def _():
        ba = ba_ref[...]
        x = ba + drow_ref[...]
        softplus = jnp.maximum(x, 0.0) + jnp.log(1.0 + jnp.exp(-jnp.abs(x)))
        g = -jnp.exp(arow_ref[...]) * softplus
        gcum = jnp.where(lane >= 3 * H_GDN, _seg_cumsum(g, True), _seg_cumsum(g, False))
        gate[...] = jnp.where(lane < 2 * H_GDN, _sigmoid(ba), gcum)

    n_chunks = t // GDN_CHUNK
    scan_refs = (gs, us, wb, qb, ab, kd, ss, os)
    intra_refs = (qs, ks, vs, gs, bs, us, wb, qb, ab, kd)

    def prologue(j):
        h = hp * hb + j
        sl = slice(j * DK, (j + 1) * DK)
        pad0 = jnp.zeros((CONV_PAD, DK), F32)
        for i, ref in enumerate((q_ref, k_ref, v_ref)):
            xp[i * hb + j, 0:CONV_PAD, :] = pad0
            xp[i * hb + j, CONV_PAD + t:, :] = pad0
            xp[i * hb + j, CONV_PAD:CONV_PAD + t, :] = ref[:, sl]
        q = _conv_silu(xp, j, t, cwq_ref, sl)
        qs[j] = q * lax.rsqrt(jnp.sum(q * q, axis=-1, keepdims=True) + RMS_EPS) * (DK ** -0.5)
        k = _conv_silu(xp, hb + j, t, cwk_ref, sl)
        ks[j] = k * lax.rsqrt(jnp.sum(k * k, axis=-1, keepdims=True) + RMS_EPS)
        vs[j] = _conv_silu(xp, 2 * hb + j, t, cwv_ref, sl)
        gt = gate[...]
        for d in range(2):
            bcol = jnp.sum(jnp.where(lane == d * H_GDN + h, gt, 0.0), axis=-1, keepdims=True)
            gcol = jnp.sum(jnp.where(lane == (2 + d) * H_GDN + h, gt, 0.0), axis=-1, keepdims=True)
            bs[j, d] = jnp.broadcast_to(bcol, (t, LANE))
            gs[j, d] = jnp.broadcast_to(gcol, (t, LANE))
            if has_s0:
                ss[j, d] = s0_ref[d, j]
        os[j] = jnp.zeros((t, DK), F32)

    def epilogue(j):
        sl = slice(j * DK, (j + 1) * DK)
        o_ref[:, sl] = (_rms_norm(os[j], nw_ref[...]) * _silu(z_ref[:, sl])).astype(o_ref.dtype)
        if has_sout:
            for d in range(2):
                sout_ref[d, j] = ss[j, d]

    def scan_items(i, js):
        items = []
        for j in js:
            for d in range(2):
                r0 = (i if d == 0 else n_chunks - 1 - i) * GDN_CHUNK
                items.append((j, d, r0 if isinstance(r0, int) else pl.multiple_of(r0, GDN_CHUNK)))
        return items

    if n_chunks <= GDN_STATIC_CHUNKS:
        def group_stages(js):
            for j in js:
                prologue(j)
                yield
            yield from _gdn_intra_stages(*intra_refs, [(j, cc * GDN_CHUNK) for j in js for cc in range(n_chunks)])
            for i in range(n_chunks):
                yield from _gdn_scan_stages(*scan_refs, scan_items(i, js), zero_state=(i == 0 and not has_s0))
            for j in js:
                epilogue(j)
                yield

        n_groups = max(hb // GDN_GROUP_HEADS, 1)
        per = hb // n_groups
        _run_skewed([group_stages(range(g * per, (g + 1) * per)) for g in range(n_groups)], GDN_SKEW)
        return

    for j in range(hb):
        prologue(j)
    cg = min(GDN_CG, n_chunks)

    def intra_group(i, carry):
        chunks = [(j, pl.multiple_of((i * cg + cc) * GDN_CHUNK, GDN_CHUNK)) for j in range(hb) for cc in range(cg)]
        _run(_gdn_intra_stages(*intra_refs, chunks))
        return carry

    def scan_step(i, carry):
        _run(_gdn_scan_stages(*scan_refs, scan_items(i, range(hb)), zero_state=False))
        return carry

    lax.fori_loop(0, n_chunks // cg, intra_group, 0)
    _run(_gdn_scan_stages(*scan_refs, scan_items(0, range(hb)), zero_state=not has_s0))
    lax.fori_loop(1, n_chunks, scan_step, 0)
    for j in range(hb):
        epilogue(j)


def _gdn(proj, conv_w, a_row, d_row, norm_w, s0, s_all, l, bsz, t):
    has_s0 = s0 is not None
    want_state = s_all is not None
    hb = GDN_HB_SHORT if t <= GDN_SHORT_T else GDN_HB
    wblk = hb * DK
    col = lambda off: (lambda b, hp: (b, off // wblk + hp))
    row0 = lambda b, hp: (0, 0)
    in_specs = [
        pl.BlockSpec((t, wblk), col(0)),
        pl.BlockSpec((t, wblk), col(GDN_W)),
        pl.BlockSpec((t, wblk), col(2 * GDN_W)),
        pl.BlockSpec((t, wblk), col(COL_Z)),
        pl.BlockSpec((t, LANE), lambda b, hp: (b, COL_BA // LANE)),
        pl.BlockSpec((1, LANE), row0),
        pl.BlockSpec((1, LANE), row0),
        pl.BlockSpec((CONV_K, wblk), lambda b, hp: (0, hp)),
        pl.BlockSpec((CONV_K, wblk), lambda b, hp: (0, GDN_W // wblk + hp)),
        pl.BlockSpec((CONV_K, wblk), lambda b, hp: (0, 2 * GDN_W // wblk + hp)),
        pl.BlockSpec((1, DK), row0),
    ]
    args = [proj, proj, proj, proj, proj, a_row, d_row, conv_w, conv_w, conv_w, norm_w]
    if has_s0:
        in_specs.append(pl.BlockSpec((None, None, 2, hb, DK, DK), lambda b, hp: (b, l, 0, hp, 0, 0)))
        args.append(s0)
    out_specs = [pl.BlockSpec((t, wblk), lambda b, hp: (b, hp))]
    out_shape = [jax.ShapeDtypeStruct((bsz * t, GDN_W), BF16)]
    aliases = {}
    if want_state:
        aliases = {len(args): 1}
        in_specs.append(pl.BlockSpec(memory_space=pl.ANY))
        args.append(s_all)
        out_specs.append(pl.BlockSpec((None, None, 2, hb, DK, DK), lambda b, hp: (b, l, 0, hp, 0, 0)))
        out_shape.append(jax.ShapeDtypeStruct(s_all.shape, F32))
    per_head = lambda dt: pltpu.VMEM((hb, t, DK), dt)
    per_dir = lambda dt: pltpu.VMEM((hb, 2, t, DK), dt)
    return pl.pallas_call(
        functools.partial(_gdn_kernel, has_s0=has_s0, has_sout=want_state),
        grid=(bsz, H_GDN // hb),
        in_specs=in_specs,
        out_specs=out_specs,
        out_shape=out_shape,
        input_output_aliases=aliases,
        scratch_shapes=[
            pltpu.VMEM((3 * hb, t + 2 * CONV_PAD, DK), F32),
            per_head(F32), per_head(F32), per_head(F32),
            pltpu.VMEM((t, LANE), F32),
            per_dir(F32), per_dir(F32),
            per_dir(F32), per_dir(BF16), per_dir(BF16), per_dir(BF16), per_dir(BF16),
            pltpu.VMEM((hb, 2, DK, DK), F32),
            per_head(F32),
        ],
        compiler_params=_cparams(("parallel", "arbitrary")),
        name="gdn_lat" if has_s0 else "gdn_ctx",
    )(*args)


def _rope(x, cos, sin):
    lane = lax.broadcasted_iota(jnp.int32, x.shape, 1)
    quarter = HEAD_DIM // 4
    partner = jnp.where((lane % (2 * quarter)) < quarter,
                        pltpu.roll(x, shift=HEAD_DIM - quarter, axis=1),
                        pltpu.roll(x, shift=quarter, axis=1))
    return x * cos + partner * sin


def _attend(qs, key_blocks, value_blocks):
    scale = HEAD_DIM ** -0.5
    scores = [[_dot_nt(q, k) * scale for k in key_blocks] for q in qs]
    probs, dens = [], []
    for sc in scores:
        m = None
        for s in sc:
            bm = jnp.max(s, axis=-1, keepdims=True)
            m = bm if m is None else jnp.maximum(m, bm)
        ps = [jnp.exp(s - m) for s in sc]
        den = None
        for p in ps:
            r = jnp.sum(p, axis=-1, keepdims=True)
            den = r if den is None else den + r
        probs.append([p.astype(BF16) for p in ps])
        dens.append(den)
    outs = []
    for ps, den in zip(probs, dens):
        acc = None
        for p, v in zip(ps, value_blocks):
            pv = _dot(p, v)
            acc = pv if acc is None else acc + pv
        outs.append(acc / den)
    return outs


def _attn_ctx_kernel(q_ref, k_ref, v_ref, qw_ref, kw_ref, kc_all_ref, vc_all_ref, o_ref, kc_ref, vc_ref):
    del kc_all_ref, vc_all_ref
    vc_ref[...] = v_ref[...]
    for g in range(N_KV):
        gl = slice(g * HEAD_DIM, (g + 1) * HEAD_DIM)
        kg = _rms_norm(k_ref[:, gl], kw_ref[...])
        kc_ref[:, gl] = kg
        heads = range(g * GRP, (g + 1) * GRP)
        qs = [_rms_norm(q_ref[:, h * HEAD_DIM:(h + 1) * HEAD_DIM], qw_ref[...]).astype(BF16) for h in heads]
        outs = _attend(qs, [kg.astype(BF16)], [v_ref[:, gl].astype(BF16)])
        for h, o in zip(heads, outs):
            o_ref[:, h * HEAD_DIM:(h + 1) * HEAD_DIM] = o.astype(o_ref.dtype)


def _attn_ctx(proj, q_norm_w, k_norm_w, kc_all, vc_all, l, bsz, t):
    cache_spec = pl.BlockSpec((None, None, t, KV_W), lambda b: (b, l, 0, 0))
    return pl.pallas_call(
        _attn_ctx_kernel,
        grid=(bsz,),
        in_specs=[
            pl.BlockSpec((t, ATTN_W), lambda b: (b, COL_Q // ATTN_W)),
            pl.BlockSpec((t, KV_W), lambda b: (b, COL_K // KV_W)),
            pl.BlockSpec((t, KV_W), lambda b: (b, COL_V // KV_W)),
            pl.BlockSpec((1, HEAD_DIM), lambda b: (0, 0)),
            pl.BlockSpec((1, HEAD_DIM), lambda b: (0, 0)),
            pl.BlockSpec(memory_space=pl.ANY),
            pl.BlockSpec(memory_space=pl.ANY),
        ],
        out_specs=[pl.BlockSpec((t, ATTN_W), lambda b: (b, 0)), cache_spec, cache_spec],
        out_shape=[
            jax.ShapeDtypeStruct((bsz * t, ATTN_W), BF16),
            jax.ShapeDtypeStruct(kc_all.shape, F32),
            jax.ShapeDtypeStruct(vc_all.shape, F32),
        ],
        input_output_aliases={5: 1, 6: 2},
        compiler_params=_cparams(("parallel",)),
        name="attn_ctx",
    )(proj, proj, proj, q_norm_w, k_norm_w, kc_all, vc_all)


def _attn_lat_kernel(q_ref, k_ref, v_ref, ck_ref, cv_ref, qw_ref, kw_ref, cq_ref, sq_ref, ck_cos_ref, ck_sin_ref,
                     o_ref):
    for g in range(N_KV):
        gl = slice(g * HEAD_DIM, (g + 1) * HEAD_DIM)
        kg = _rope(_rms_norm(k_ref[:, gl], kw_ref[...]), ck_cos_ref[...], ck_sin_ref[...]).astype(BF16)
        heads = range(g * GRP, (g + 1) * GRP)
        qs = [_rope(_rms_norm(q_ref[:, h * HEAD_DIM:(h + 1) * HEAD_DIM], qw_ref[...]), cq_ref[...], sq_ref[...])
              .astype(BF16) for h in heads]
        outs = _attend(qs, [kg, ck_ref[:, gl].astype(BF16)], [v_ref[:, gl].astype(BF16), cv_ref[:, gl].astype(BF16)])
        for h, o in zip(heads, outs):
            o_ref[:, h * HEAD_DIM:(h + 1) * HEAD_DIM] = o.astype(o_ref.dtype)


def _attn_lat(proj, cache_k, cache_v, q_norm_w, k_norm_w, cos_t, sin_t, l, bsz, t, tq):
    past = cache_k.shape[2]
    nq = t // tq
    return pl.pallas_call(
        _attn_lat_kernel,
        grid=(bsz, nq),
        in_specs=[
            pl.BlockSpec((tq, ATTN_W), lambda b, i: (b * nq + i, COL_Q // ATTN_W)),
            pl.BlockSpec((t, KV_W), lambda b, i: (b, COL_K // KV_W)),
            pl.BlockSpec((t, KV_W), lambda b, i: (b, COL_V // KV_W)),
            pl.BlockSpec((None, None, past, KV_W), lambda b, i: (b, l, 0, 0)),
            pl.BlockSpec((None, None, past, KV_W), lambda b, i: (b, l, 0, 0)),
            pl.BlockSpec((1, HEAD_DIM), lambda b, i: (0, 0)),
            pl.BlockSpec((1, HEAD_DIM), lambda b, i: (0, 0)),
            pl.BlockSpec((tq, HEAD_DIM), lambda b, i: (i, 0)),
            pl.BlockSpec((tq, HEAD_DIM), lambda b, i: (i, 0)),
            pl.BlockSpec((t, HEAD_DIM), lambda b, i: (0, 0)),
            pl.BlockSpec((t, HEAD_DIM), lambda b, i: (0, 0)),
        ],
        out_specs=pl.BlockSpec((tq, ATTN_W), lambda b, i: (b * nq + i, 0)),
        out_shape=jax.ShapeDtypeStruct((bsz * t, ATTN_W), BF16),
        compiler_params=_cparams(("parallel", "arbitrary")),
        name="attn_lat",
    )(proj, proj, proj, cache_k, cache_v, q_norm_w, k_norm_w, cos_t, sin_t, cos_t, sin_t)


def _rope_lane_tables(n_tok):
    rows = n_tok // GRID_W
    row = jnp.repeat(jnp.arange(rows), GRID_W).astype(F32)
    colp = jnp.tile(jnp.arange(GRID_W), rows).astype(F32)
    half = HEAD_DIM // 2
    inv_freq = ROPE_BASE ** (-jnp.arange(0, half, 2, dtype=F32) / half)
    ang_r = row[:, None] * inv_freq
    ang_c = colp[:, None] * inv_freq
    cos_t = jnp.concatenate([jnp.cos(ang_r), jnp.cos(ang_r), jnp.cos(ang_c), jnp.cos(ang_c)], axis=-1)
    sin_t = jnp.concatenate([-jnp.sin(ang_r), jnp.sin(ang_r), -jnp.sin(ang_c), jnp.sin(ang_c)], axis=-1)
    return cos_t, sin_t


def _outproj_kernel(og_ref, oa_ref, wg_ref, wa_ref, x_ref, m_ref, g_ref, b_ref, o_ref):
    mix = _dot(og_ref[...], wg_ref[...]) + _dot(oa_ref[...], wa_ref[...])
    y = DN_ALPHA * x_ref[...] + m_ref[2:3, :] * mix
    o_ref[...] = _layer_norm(y, g_ref[...], b_ref[...])


def _out_proj(o_gdn, o_att, w_o, l, x, mod, ln_g, ln_b, tiles_per_group):
    n = x.shape[0]
    row = lambda i: (i, 0)
    const = lambda i: (0, 0)
    return pl.pallas_call(
        _outproj_kernel,
        grid=(n // TM,),
        in_specs=[
            pl.BlockSpec((TM, GDN_W), row),
            pl.BlockSpec((TM, ATTN_W), row),
            pl.BlockSpec((None, GDN_W, D_MODEL), lambda i: (l, 0, 0)),
            pl.BlockSpec((None, ATTN_W, D_MODEL), lambda i: (l, 1, 0)),
            pl.BlockSpec((TM, D_MODEL), row),
            pl.BlockSpec((None, 6, D_MODEL), lambda i: (i // tiles_per_group, 0, 0)),
            pl.BlockSpec((1, D_MODEL), const),
            pl.BlockSpec((1, D_MODEL), const),
        ],
        out_specs=pl.BlockSpec((TM, D_MODEL), row),
        out_shape=jax.ShapeDtypeStruct((n, D_MODEL), F32),
        compiler_params=_cparams(("parallel",)),
        name="out_proj",
    )(o_gdn, o_att, w_o, w_o, x, mod, ln_g, ln_b)


def _ffn_up_kernel(x_ref, m_ref, wg_ref, wu_ref, o_ref, h_ref):
    @pl.when(pl.program_id(1) == 0)
    def _():
        h_ref[...] = (x_ref[...] * (1.0 + m_ref[4:5, :]) + m_ref[3:4, :]).astype(BF16)

    h = h_ref[...]
    gate = _dot(h, wg_ref[...].astype(BF16))
    up = _dot(h, wu_ref[...].astype(BF16))
    o_ref[...] = (_silu(gate) * up).astype(o_ref.dtype)


def _ffn_up(x, mod, w_gu, l, rows_per_group):
    n = x.shape[0]
    nj = D_FF // TN_FF
    tm = min(TM_UP, rows_per_group)
    tiles_per_group = rows_per_group // tm
    return pl.pallas_call(
        _ffn_up_kernel,
        grid=(n // tm, nj),
        in_specs=[
            pl.BlockSpec((tm, D_MODEL), lambda i, j: (i, 0)),
            pl.BlockSpec((None, 6, D_MODEL), lambda i, j: (i // tiles_per_group, 0, 0)),
            pl.BlockSpec((None, D_MODEL, TN_FF), lambda i, j: (l, 0, j)),
            pl.BlockSpec((None, D_MODEL, TN_FF), lambda i, j: (l, 0, nj + j)),
        ],
        out_specs=pl.BlockSpec((tm, TN_FF), lambda i, j: (i, j)),
        out_shape=jax.ShapeDtypeStruct((n, D_FF), BF16),
        scratch_shapes=[pltpu.VMEM((tm, D_MODEL), BF16)],
        compiler_params=_cparams(("parallel", "arbitrary")),
        name="ffn_up",
    )(x, mod, w_gu, w_gu)


def _ffn_down_kernel(a_ref, w_ref, x_ref, m_ref, g_ref, b_ref, o_ref, acc_ref):
    k = pl.program_id(1)

    part = _dot(a_ref[...], w_ref[...])

    last = pl.num_programs(1) - 1

    @pl.when(k == 0)
    def _():
        acc_ref[...] = part

    @pl.when((k > 0) & (k < last))
    def _():
        acc_ref[...] += part

    @pl.when(k == last)
    def _():
        y = DN_ALPHA * x_ref[...] + m_ref[5:6, :] * (acc_ref[...] + part)
        o_ref[...] = _layer_norm(y, g_ref[...], b_ref[...])


def _ffn_down(act, w_down, l, x, mod, ln_g, ln_b, tiles_per_group):
    n = x.shape[0]
    return pl.pallas_call(
        _ffn_down_kernel,
        grid=(n // TM, D_FF // TK_FF),
        in_specs=[
            pl.BlockSpec((TM, TK_FF), lambda i, k: (i, k)),
            pl.BlockSpec((None, TK_FF, D_MODEL), lambda i, k: (l, k, 0)),
            pl.BlockSpec((TM, D_MODEL), lambda i, k: (i, 0)),
            pl.BlockSpec((None, 6, D_MODEL), lambda i, k: (i // tiles_per_group, 0, 0)),
            pl.BlockSpec((1, D_MODEL), lambda i, k: (0, 0)),
            pl.BlockSpec((1, D_MODEL), lambda i, k: (0, 0)),
        ],
        out_specs=pl.BlockSpec((TM, D_MODEL), lambda i, k: (i, 0)),
        out_shape=jax.ShapeDtypeStruct((n, D_MODEL), F32),
        scratch_shapes=[pltpu.VMEM((TM, D_MODEL), F32)],
        compiler_params=_cparams(("parallel", "arbitrary")),
        name="ffn_down",
    )(act, w_down, x, mod, ln_g, ln_b)


def _trunk_layer(x, mod, lw, l, bsz, t, latent, ctx_out):
    assert t % TM == 0 or TM % t == 0
    tiles_per_group = max(t // TM, 1) if latent is not None else x.shape[0] // TM
    proj = _in_proj(x, mod, lw["w_in"], l, tiles_per_group * TM)
    if latent is None:
        kc_all, vc_all, s_all = ctx_out
        o_gdn, s_all = _gdn(proj, lw["conv_w"], lw["a_row"], lw["d_row"], lw["gdn_norm_w"], None, s_all, l, bsz, t)
        o_att, kc_all, vc_all = _attn_ctx(proj, lw["q_norm_w"], lw["k_norm_w"], kc_all, vc_all, l, bsz, t)
        ctx_out = (kc_all, vc_all, s_all)
    else:
        cache_k, cache_v, state_gdn, cos_t, sin_t = latent
        (o_gdn,) = _gdn(proj, lw["conv_w"], lw["a_row"], lw["d_row"], lw["gdn_norm_w"], state_gdn, None, l, bsz, t)
        o_att = _attn_lat(proj, cache_k, cache_v, lw["q_norm_w"], lw["k_norm_w"], cos_t, sin_t, l, bsz, t, 256)
    x1 = _out_proj(o_gdn, o_att, lw["w_o"], l, x, mod, lw["ln1_g"], lw["ln1_b"], tiles_per_group)
    act = _ffn_up(x1, mod, lw["w_gate_up"], l, tiles_per_group * TM)
    x2 = _ffn_down(act, lw["w_down"], l, x1, mod, lw["ln2_g"], lw["ln2_b"], tiles_per_group)
    return x2, ctx_out


def kernel(x_prompt, x_sample, cache_k, cache_v, state_gdn, c, c_ctx, w_ada, b_ada, w_in, conv_w, a_log, dt_bias,
           gdn_norm_w, q_norm_w, k_norm_w, w_o, ln1_g, ln1_b, ln2_g, ln2_b, w_gate_up, w_down):
    bsz, seq, _ = x_prompt.shape
    dbsz, dseq, _ = x_sample.shape
    past = cache_k.shape[2]

    i_b = QKV_W + GDN_W
    i_q = i_b + 4 * H_GDN
    w_in_r = jnp.concatenate(
        [w_in[:, :, :i_b], w_in[:, :, i_q:], w_in[:, :, i_b:i_q],
         jnp.zeros((DEPTH, D_MODEL, LANE - 4 * H_GDN), w_in.dtype)], axis=-1).astype(BF16)
    w_o_b = w_o.astype(BF16)
    w_down_b = w_down.astype(BF16)

    cond8 = jnp.concatenate([c_ctx[None, :], c, jnp.zeros((8 - 1 - dbsz, D_MODEL), F32)], axis=0)
    mod = _ada(cond8, w_ada, b_ada.reshape(DEPTH, 1, 6 * D_MODEL)).reshape(DEPTH, 8, 6, D_MODEL)

    cos_t, sin_t = _rope_lane_tables(dseq)
    ck = cache_k.reshape(dbsz, DEPTH, past, KV_W)
    cv = cache_v.reshape(dbsz, DEPTH, past, KV_W)

    def gate_rows(p):
        return jnp.zeros((1, LANE), F32).at[0, 2 * H_GDN:4 * H_GDN].set(p.reshape(-1))

    xp = x_prompt.reshape(bsz * seq, D_MODEL)
    xs = x_sample.reshape(dbsz * dseq, D_MODEL)
    ctx_out = (jnp.zeros((bsz, DEPTH, seq, KV_W), F32), jnp.zeros((bsz, DEPTH, seq, KV_W), F32),
               jnp.zeros((bsz, DEPTH, 2, H_GDN, DK, DK), F32))
    for l in range(DEPTH):
        lw = {
            "w_in": w_in_r, "conv_w": conv_w[l], "a_row": gate_rows(a_log[l]), "d_row": gate_rows(dt_bias[l]),
            "gdn_norm_w": gdn_norm_w[l].reshape(1, DK), "q_norm_w": q_norm_w[l].reshape(1, HEAD_DIM),
            "k_norm_w": k_norm_w[l].reshape(1, HEAD_DIM), "w_o": w_o_b,
            "ln1_g": ln1_g[l].reshape(1, D_MODEL), "ln1_b": ln1_b[l].reshape(1, D_MODEL),
            "ln2_g": ln2_g[l].reshape(1, D_MODEL), "ln2_b": ln2_b[l].reshape(1, D_MODEL),
            "w_gate_up": w_gate_up, "w_down": w_down_b,
        }
        xp, ctx_out = _trunk_layer(xp, mod[l, 0:1], lw, l, bsz, seq, None, ctx_out)
        xs, _ = _trunk_layer(xs, mod[l, 1:1 + dbsz], lw, l, dbsz, dseq, (ck, cv, state_gdn, cos_t, sin_t), None)
    kc_all, vc_all, s_all = ctx_out
    return (xp.reshape(bsz, seq, D_MODEL), xs.reshape(dbsz, dseq, D_MODEL),
            kc_all.reshape(bsz, DEPTH, seq, N_KV, HEAD_DIM), vc_all.reshape(bsz, DEPTH, seq, N_KV, HEAD_DIM), s_all)
```

```python
import functools
import math

import jax
import jax.numpy as jnp
from jax import lax
from jax.experimental import pallas as pl
from jax.experimental.pallas import tpu as pltpu

D_MODEL = 2048
DEPTH = 2
GRID_W = 64
DK = 128
H_GDN = 8
GDN_W = H_GDN * DK
HEAD_DIM = 128
N_Q = 8
N_KV = 2
GRP = N_Q // N_KV
ATTN_W = N_Q * HEAD_DIM
KV_W = N_KV * HEAD_DIM
CONV_K = 5
D_FF = 5632
ROPE_BASE = 10000.0
RMS_EPS = 1e-6
LN_EPS = 1e-5
DN_ALPHA = (2 * DEPTH) ** 0.25
QKV_W = 3 * GDN_W
IN_W = QKV_W + GDN_W + 4 * H_GDN + ATTN_W + 2 * KV_W

LANE = 128
GDN_CHUNK = 128
GDN_HB = 2
GDN_HB_SHORT = 8
GDN_SHORT_T = 256
GDN_CG = 4
CONV_PAD = 8
GDN_STATIC_CHUNKS = 8
GDN_GROUP_HEADS = 4
GDN_SKEW = 12
COL_Z = QKV_W
HEAD_W = COL_Z + GDN_W
COL_Q = 0
COL_K = COL_Q + ATTN_W
COL_V = COL_K + KV_W
COL_BA = COL_V + KV_W
TAIL_W = COL_BA + LANE

TM = 512
TM_UP = 1024
TM_IN = 1024
TN_HEAD = 1024
TN_FF = 512
TK_FF = 2816
TN_ADA = 1024
VMEM_LIMIT = 56 * 1024 * 1024

F32 = jnp.float32
BF16 = jnp.bfloat16


def _cparams(sem):
    return pltpu.CompilerParams(dimension_semantics=sem, vmem_limit_bytes=VMEM_LIMIT)


def _dot(a, b):
    return jnp.dot(a, b, preferred_element_type=F32)


def _dot_nt(a, b):
    return lax.dot_general(a, b, (((1,), (1,)), ((), ())), preferred_element_type=F32)


def _dot_tn(a, b):
    return lax.dot_general(a, b, (((0,), (0,)), ((), ())), preferred_element_type=F32)


def _sigmoid(x):
    return 1.0 / (1.0 + jnp.exp(-x))


def _silu(x):
    return x * _sigmoid(x)


def _layer_norm(y, g, b):
    mu = jnp.mean(y, axis=-1, keepdims=True)
    yc = y - mu
    var = jnp.mean(yc * yc, axis=-1, keepdims=True)
    return yc * lax.rsqrt(var + LN_EPS) * g + b


def _rms_norm(x, w):
    return x * lax.rsqrt(jnp.mean(x * x, axis=-1, keepdims=True) + RMS_EPS) * w


def _ada_kernel(c_ref, w_ref, b_ref, o_ref):
    c = c_ref[...]
    o_ref[...] = _dot(_silu(c).astype(BF16), w_ref[...].astype(BF16)) + b_ref[...]


def _ada(cond8, w_ada, b_ada):
    n = w_ada.shape[-1]
    return pl.pallas_call(
        _ada_kernel,
        grid=(DEPTH, n // TN_ADA),
        in_specs=[
            pl.BlockSpec((8, D_MODEL), lambda l, j: (0, 0)),
            pl.BlockSpec((None, D_MODEL, TN_ADA), lambda l, j: (l, 0, j)),
            pl.BlockSpec((None, 1, TN_ADA), lambda l, j: (l, 0, j)),
        ],
        out_specs=pl.BlockSpec((None, 8, TN_ADA), lambda l, j: (l, 0, j)),
        out_shape=jax.ShapeDtypeStruct((DEPTH, 8, n), F32),
        compiler_params=_cparams(("arbitrary", "arbitrary")),
        name="ada",
    )(cond8, w_ada, b_ada)


def _inproj_kernel(x_ref, m_ref, w_ref, o_ref, h_ref):
    @pl.when(pl.program_id(1) == 0)
    def _():
        h_ref[...] = (x_ref[...] * (1.0 + m_ref[1:2, :]) + m_ref[0:1, :]).astype(BF16)

    o_ref[...] = _dot(h_ref[...], w_ref[...].astype(BF16))


def _in_proj(x, mod, w, l, rows_per_group, n_out, tn):
    n = x.shape[0]
    tm = min(TM_IN, rows_per_group)
    tiles_per_group = rows_per_group // tm
    return pl.pallas_call(
        _inproj_kernel,
        grid=(n // tm, n_out // tn),
        in_specs=[
            pl.BlockSpec((tm, D_MODEL), lambda i, j: (i, 0)),
            pl.BlockSpec((None, 6, D_MODEL), lambda i, j: (i // tiles_per_group, 0, 0)),
            pl.BlockSpec((None, D_MODEL, tn), lambda i, j: (l, 0, j)),
        ],
        out_specs=pl.BlockSpec((tm, tn), lambda i, j: (i, j)),
        out_shape=jax.ShapeDtypeStruct((n, n_out), F32),
        scratch_shapes=[pltpu.VMEM((tm, D_MODEL), BF16)],
        compiler_params=_cparams(("parallel", "arbitrary")),
        name="in_proj",
    )(x, mod, w)


def _conv_silu(xp, i, t, w_ref, cols):
    acc = None
    for j in range(CONV_K):
        start = CONV_PAD - CONV_K // 2 + j
        term = xp[i, start:start + t, :] * w_ref[j:j + 1, cols]
        acc = term if acc is None else acc + term
    return _silu(acc)


def _seg_cumsum(x, reverse):
    t = x.shape[0]
    pos = lax.broadcasted_iota(jnp.int32, x.shape, 0) % GDN_CHUNK
    s = 1
    while s < GDN_CHUNK:
        if reverse:
            x = x + jnp.where(pos < GDN_CHUNK - s, pltpu.roll(x, shift=t - s, axis=0), 0.0)
        else:
            x = x + jnp.where(pos >= s, pltpu.roll(x, shift=s, axis=0), 0.0)
        s *= 2
    return x


def _unit_tri_inverses(lms, out):
    c = lms[0].shape[0]
    row = lax.broadcasted_iota(jnp.int32, (c, c), 0)
    col = lax.broadcasted_iota(jnp.int32, (c, c), 1)
    rc = row ^ col
    eye = (row == col).astype(F32)
    zero = jnp.zeros((c, c), BF16)
    xs = [eye - jnp.where(rc == 1, lm, 0.0) for lm in lms]
    lbs = [lm.astype(BF16) for lm in lms]
    for lvl in range(1, int(math.log2(c))):
        keep = (rc >> lvl) == 1
        xbs = [x.astype(BF16) for x in xs]
        ys = [_dot(xb, jnp.where(keep, lb, zero)).astype(BF16) for xb, lb in zip(xbs, lbs)]
        yield
        xs = [x - _dot(y, xb) for x, y, xb in zip(xs, ys, xbs)]
        yield
    out.extend(xs)


def _gdn_intra_stages(qs, ks, vs, gs, bs, us, wb, qb, ab, kd, chunks):
    c = GDN_CHUNK
    row = lax.broadcasted_iota(jnp.int32, (c, c), 0)
    col = lax.broadcasted_iota(jnp.int32, (c, c), 1)
    incl = (row >= col, row <= col)
    strict = (row > col, row < col)
    kqs = []
    for j, r0 in chunks:
        rows = pl.ds(r0, c)
        kcb = ks[j, rows, :].astype(BF16)
        kqs.append(_dot_nt(jnp.concatenate([kcb, qs[j, rows, :].astype(BF16)], axis=0), kcb))
    yield
    work, lms = [], []
    for (j, r0), kq in zip(chunks, kqs):
        rows = pl.ds(r0, c)
        for d in range(2):
            gcum = gs[j, d, rows, :]
            beta = bs[j, d, rows, :]
            dm = gcum - gcum.T
            decay = jnp.where(incl[d], jnp.exp(jnp.where(incl[d], dm, 0.0)), 0.0)
            lms.append(jnp.where(strict[d], kq[:c] * beta * decay, 0.0))
            ab[j, d, rows, :] = (kq[c:] * decay).astype(BF16)
            work.append((j, d, rows, gcum, beta))
    xs = []
    yield from _unit_tri_inverses(lms, xs)
    for (j, d, rows, gcum, beta), x in zip(work, xs):
        qc = qs[j, rows, :]
        kc = ks[j, rows, :]
        egc = jnp.exp(gcum)
        rhs = jnp.concatenate([vs[j, rows, :] * beta, kc * (beta * egc)], axis=1)
        sol = _dot(x.astype(BF16), rhs.astype(BF16))
        g_last = gcum[c - 1:c, :] if d == 0 else gcum[0:1, :]
        us[j, d, rows, :] = sol[:, :DK]
        wb[j, d, rows, :] = sol[:, DK:].astype(BF16)
        qb[j, d, rows, :] = (qc * egc).astype(BF16)
        kd[j, d, rows, :] = (kc * jnp.exp(g_last - gcum)).astype(BF16)
    yield


def _gdn_scan_stages(gs, us, wb, qb, ab, kd, ss, os, items, zero_state):
    c = GDN_CHUNK
    mids = []
    for j, d, r0 in items:
        rows = pl.ds(r0, c)
        if zero_state:
            mids.append((us[j, d, rows, :].astype(BF16), None, None))
        else:
            s_state = ss[j, d]
            ws_qs = _dot(jnp.concatenate([wb[j, d, rows, :], qb[j, d, rows, :]], axis=0), s_state.astype(BF16))
            mids.append(((us[j, d, rows, :] - ws_qs[:c]).astype(BF16), ws_qs[c:], s_state))
    yield
    for (j, d, r0), (v_new, q_s, s_state) in zip(items, mids):
        rows = pl.ds(r0, c)
        o = _dot(ab[j, d, rows, :], v_new)
        os[j, rows, :] += o if zero_state else q_s + o
        upd = _dot_tn(kd[j, d, rows, :], v_new)
        if zero_state:
            ss[j, d] = upd
        else:
            g_last = gs[j, d, pl.ds(r0 + (c - 1 if d == 0 else 0), 1), :]
            ss[j, d] = s_state * jnp.exp(g_last) + upd
    yield


def _run(gen):
    for _ in gen:
        pass


def _run_skewed(gens, skew):
    live, started, rnd = [], 0, 0
    while started < len(gens) or live:
        if started < len(gens) and rnd >= started * skew:
            live.append(gens[started])
            started += 1
        for g in list(live):
            try:
                next(g)
            except StopIteration:
                live.remove(g)
        rnd += 1


def _gdn_kernel(q_ref, k_ref, v_ref, z_ref, ba_ref, arow_ref, drow_ref, cwq_ref, cwk_ref, cwv_ref, nw_ref, *rest,
                has_s0, has_sout):
    rest = list(rest)
    s0_ref = rest.pop(0) if has_s0 else None
    if has_sout:
        rest.pop(0)
    o_ref = rest.pop(0)
    sout_ref = rest.pop(0) if has_sout else None
    xp, qs, ks, vs, gate, gs, bs, us, wb, qb, ab, kd, ss, os = rest
    hp = pl.program_id(1)
    t = q_ref.shape[0]
    hb = q_ref.shape[1] // DK
    lane = lax.broadcasted_iota(jnp.int32, (t, LANE), 1)

    @pl.when(hp == 0)
    def _():
        ba = ba_ref[...]
        x = ba + drow_ref[...]
        softplus = jnp.maximum(x, 0.0) + jnp.log(1.0 + jnp.exp(-jnp.abs(x)))
        g = -jnp.exp(arow_ref[...]) * softplus
        gcum = jnp.where(lane >= 3 * H_GDN, _seg_cumsum(g, True), _seg_cumsum(g, False))
        gate[...] = jnp.where(lane < 2 * H_GDN, _sigmoid(ba), gcum)

    n_chunks = t // GDN_CHUNK
    scan_refs = (gs, us, wb, qb, ab, kd, ss, os)
    intra_refs = (qs, ks, vs, gs, bs, us, wb, qb, ab, kd)

    def prologue(j):
        h = hp * hb + j
        sl = slice(j * DK, (j + 1) * DK)
        pad0 = jnp.zeros((CONV_PAD, DK), F32)
        for i, ref in enumerate((q_ref, k_ref, v_ref)):
            xp[i * hb + j, 0:CONV_PAD, :] = pad0
            xp[i * hb + j, CONV_PAD + t:, :] = pad0
            xp[i * hb + j, CONV_PAD:CONV_PAD + t, :] = ref[:, sl]
        q = _conv_silu(xp, j, t, cwq_ref, sl)
        qs[j] = q * lax.rsqrt(jnp.sum(q * q, axis=-1, keepdims=True) + RMS_EPS) * (DK ** -0.5)
        k = _conv_silu(xp, hb + j, t, cwk_ref, sl)
        ks[j] = k * lax.rsqrt(jnp.sum(k * k, axis=-1, keepdims=True) + RMS_EPS)
        vs[j] = _conv_silu(xp, 2 * hb + j, t, cwv_ref, sl)
        gt = gate[...]
        for d in range(2):
            bcol = jnp.sum(jnp.where(lane == d * H_GDN + h, gt, 0.0), axis=-1, keepdims=True)
            gcol = jnp.sum(jnp.where(lane == (2 + d) * H_GDN + h, gt, 0.0), axis=-1, keepdims=True)
            bs[j, d] = jnp.broadcast_to(bcol, (t, LANE))
            gs[j, d] = jnp.broadcast_to(gcol, (t, LANE))
            if has_s0:
                ss[j, d] = s0_ref[d, j]
        os[j] = jnp.zeros((t, DK), F32)

    def epilogue(j):
        sl = slice(j * DK, (j + 1) * DK)
        o_ref[:, sl] = (_rms_norm(os[j], nw_ref[...]) * _silu(z_ref[:, sl])).astype(o_ref.dtype)
        if has_sout:
            for d in range(2):
                sout_ref[d, j] = ss[j, d]

    def scan_items(i, js):
        items = []
        for j in js:
            for d in range(2):
                r0 = (i if d == 0 else n_chunks - 1 - i) * GDN_CHUNK
                items.append((j, d, r0 if isinstance(r0, int) else pl.multiple_of(r0, GDN_CHUNK)))
        return items

    if n_chunks <= GDN_STATIC_CHUNKS:
        def group_stages(js):
            for j in js:
                prologue(j)
                yield
            yield from _gdn_intra_stages(*intra_refs, [(j, cc * GDN_CHUNK) for j in js for cc in range(n_chunks)])
            for i in range(n_chunks):
                yield from _gdn_scan_stages(*scan_refs, scan_items(i, js), zero_state=(i == 0 and not has_s0))
            for j in js:
                epilogue(j)
                yield

        per = min(GDN_GROUP_HEADS, max(hb // 2, 1))
        n_groups = hb // per
        _run_skewed([group_stages(range(g * per, (g + 1) * per)) for g in range(n_groups)], GDN_SKEW)
        return

    for j in range(hb):
        prologue(j)
    cg = min(GDN_CG, n_chunks)

    def intra_group(i, carry):
        chunks = [(j, pl.multiple_of((i * cg + cc) * GDN_CHUNK, GDN_CHUNK)) for j in range(hb) for cc in range(cg)]
        _run(_gdn_intra_stages(*intra_refs, chunks))
        return carry

    def scan_step(i, carry):
        _run(_gdn_scan_stages(*scan_refs, scan_items(i, range(hb)), zero_state=False))
        return carry

    lax.fori_loop(0, n_chunks // cg, intra_group, 0)
    _run(_gdn_scan_stages(*scan_refs, scan_items(0, range(hb)), zero_state=not has_s0))
    lax.fori_loop(1, n_chunks, scan_step, 0)
    for j in range(hb):
        epilogue(j)


def _gdn(proj, proj_tail, conv_w, a_row, d_row, norm_w, s0, s_all, l, bsz, t):
    has_s0 = s0 is not None
    want_state = s_all is not None
    hb = GDN_HB_SHORT if t <= GDN_SHORT_T else GDN_HB
    wblk = hb * DK
    col = lambda off: (lambda b, hp: (b, off // wblk + hp))
    row0 = lambda b, hp: (0, 0)
    in_specs = [
        pl.BlockSpec((t, wblk), col(0)),
        pl.BlockSpec((t, wblk), col(GDN_W)),
        pl.BlockSpec((t, wblk), col(2 * GDN_W)),
        pl.BlockSpec((t, wblk), col(COL_Z)),
        pl.BlockSpec((t, LANE), lambda b, hp: (b, COL_BA // LANE)),
        pl.BlockSpec((1, LANE), row0),
        pl.BlockSpec((1, LANE), row0),
        pl.BlockSpec((CONV_K, wblk), lambda b, hp: (0, hp)),
        pl.BlockSpec((CONV_K, wblk), lambda b, hp: (0, GDN_W // wblk + hp)),
        pl.BlockSpec((CONV_K, wblk), lambda b, hp: (0, 2 * GDN_W // wblk + hp)),
        pl.BlockSpec((1, DK), row0),
    ]
    args = [proj, proj, proj, proj, proj_tail, a_row, d_row, conv_w, conv_w, conv_w, norm_w]
    if has_s0:
        in_specs.append(pl.BlockSpec((None, None, 2, hb, DK, DK), lambda b, hp: (b, l, 0, hp, 0, 0)))
        args.append(s0)
    out_specs = [pl.BlockSpec((t, wblk), lambda b, hp: (b, hp))]
    out_shape = [jax.ShapeDtypeStruct((bsz * t, GDN_W), BF16)]
    aliases = {}
    if want_state:
        aliases = {len(args): 1}
        in_specs.append(pl.BlockSpec(memory_space=pl.ANY))
        args.append(s_all)
        out_specs.append(pl.BlockSpec((None, None, 2, hb, DK, DK), lambda b, hp: (b, l, 0, hp, 0, 0)))
        out_shape.append(jax.ShapeDtypeStruct(s_all.shape, F32))
    per_head = lambda dt: pltpu.VMEM((hb, t, DK), dt)
    per_dir = lambda dt: pltpu.VMEM((hb, 2, t, DK), dt)
    return pl.pallas_call(
        functools.partial(_gdn_kernel, has_s0=has_s0, has_sout=want_state),
        grid=(bsz, H_GDN // hb),
        in_specs=in_specs,
        out_specs=out_specs,
        out_shape=out_shape,
        input_output_aliases=aliases,
        scratch_shapes=[
            pltpu.VMEM((3 * hb, t + 2 * CONV_PAD, DK), F32),
            per_head(F32), per_head(F32), per_head(F32),
            pltpu.VMEM((t, LANE), F32),
            per_dir(F32), per_dir(F32),
            per_dir(F32), per_dir(BF16), per_dir(BF16), per_dir(BF16), per_dir(BF16),
            pltpu.VMEM((hb, 2, DK, DK), F32),
            per_head(F32),
        ],
        compiler_params=_cparams(("parallel", "arbitrary")),
        name="gdn_lat" if has_s0 else "gdn_ctx",
    )(*args)


def _rope(x, cos, sin):
    lane = lax.broadcasted_iota(jnp.int32, x.shape, 1)
    quarter = HEAD_DIM // 4
    partner = jnp.where((lane % (2 * quarter)) < quarter,
                        pltpu.roll(x, shift=HEAD_DIM - quarter, axis=1),
                        pltpu.roll(x, shift=quarter, axis=1))
    return x * cos + partner * sin


def _attend(qs, key_blocks, value_blocks):
    scale = HEAD_DIM ** -0.5
    scores = [[_dot_nt(q, k) * scale for k in key_blocks] for q in qs]
    probs, dens = [], []
    for sc in scores:
        m = None
        for s in sc:
            bm = jnp.max(s, axis=-1, keepdims=True)
            m = bm if m is None else jnp.maximum(m, bm)
        ps = [jnp.exp(s - m) for s in sc]
        den = None
        for p in ps:
            r = jnp.sum(p, axis=-1, keepdims=True)
            den = r if den is None else den + r
        probs.append([p.astype(BF16) for p in ps])
        dens.append(den)
    outs = []
    for ps, den in zip(probs, dens):
        acc = None
        for p, v in zip(ps, value_blocks):
            pv = _dot(p, v)
            acc = pv if acc is None else acc + pv
        outs.append(acc / den)
    return outs


def _attn_ctx_kernel(q_ref, k_ref, v_ref, qw_ref, kw_ref, kc_all_ref, vc_all_ref, o_ref, kc_ref, vc_ref):
    del kc_all_ref, vc_all_ref
    vc_ref[...] = v_ref[...]
    for g in range(N_KV):
        gl = slice(g * HEAD_DIM, (g + 1) * HEAD_DIM)
        kg = _rms_norm(k_ref[:, gl], kw_ref[...])
        kc_ref[:, gl] = kg
        heads = range(g * GRP, (g + 1) * GRP)
        qs = [_rms_norm(q_ref[:, h * HEAD_DIM:(h + 1) * HEAD_DIM], qw_ref[...]).astype(BF16) for h in heads]
        outs = _attend(qs, [kg.astype(BF16)], [v_ref[:, gl].astype(BF16)])
        for h, o in zip(heads, outs):
            o_ref[:, h * HEAD_DIM:(h + 1) * HEAD_DIM] = o.astype(o_ref.dtype)


def _attn_ctx(proj, q_norm_w, k_norm_w, kc_all, vc_all, l, bsz, t):
    cache_spec = pl.BlockSpec((None, None, t, KV_W), lambda b: (b, l, 0, 0))
    return pl.pallas_call(
        _attn_ctx_kernel,
        grid=(bsz,),
        in_specs=[
            pl.BlockSpec((t, ATTN_W), lambda b: (b, COL_Q // ATTN_W)),
            pl.BlockSpec((t, KV_W), lambda b: (b, COL_K // KV_W)),
            pl.BlockSpec((t, KV_W), lambda b: (b, COL_V // KV_W)),
            pl.BlockSpec((1, HEAD_DIM), lambda b: (0, 0)),
            pl.BlockSpec((1, HEAD_DIM), lambda b: (0, 0)),
            pl.BlockSpec(memory_space=pl.ANY),
            pl.BlockSpec(memory_space=pl.ANY),
        ],
        out_specs=[pl.BlockSpec((t, ATTN_W), lambda b: (b, 0)), cache_spec, cache_spec],
        out_shape=[
            jax.ShapeDtypeStruct((bsz * t, ATTN_W), BF16),
            jax.ShapeDtypeStruct(kc_all.shape, F32),
            jax.ShapeDtypeStruct(vc_all.shape, F32),
        ],
        input_output_aliases={5: 1, 6: 2},
        compiler_params=_cparams(("parallel",)),
        name="attn_ctx",
    )(proj, proj, proj, q_norm_w, k_norm_w, kc_all, vc_all)


def _attn_lat_kernel(q_ref, k_ref, v_ref, ck_ref, cv_ref, qw_ref, kw_ref, cq_ref, sq_ref, ck_cos_ref, ck_sin_ref,
                     o_ref):
    for g in range(N_KV):
        gl = slice(g * HEAD_DIM, (g + 1) * HEAD_DIM)
        kg = _rope(_rms_norm(k_ref[:, gl], kw_ref[...]), ck_cos_ref[...], ck_sin_ref[...]).astype(BF16)
        heads = range(g * GRP, (g + 1) * GRP)
        qs = [_rope(_rms_norm(q_ref[:, h * HEAD_DIM:(h + 1) * HEAD_DIM], qw_ref[...]), cq_ref[...], sq_ref[...])
              .astype(BF16) for h in heads]
        outs = _attend(qs, [kg, ck_ref[:, gl].astype(BF16)], [v_ref[:, gl].astype(BF16), cv_ref[:, gl].astype(BF16)])
        for h, o in zip(heads, outs):
            o_ref[:, h * HEAD_DIM:(h + 1) * HEAD_DIM] = o.astype(o_ref.dtype)


def _attn_lat(proj, cache_k, cache_v, q_norm_w, k_norm_w, cos_t, sin_t, l, bsz, t, tq):
    past = cache_k.shape[2]
    nq = t // tq
    return pl.pallas_call(
        _attn_lat_kernel,
        grid=(bsz, nq),
        in_specs=[
            pl.BlockSpec((tq, ATTN_W), lambda b, i: (b * nq + i, COL_Q // ATTN_W)),
            pl.BlockSpec((t, KV_W), lambda b, i: (b, COL_K // KV_W)),
            pl.BlockSpec((t, KV_W), lambda b, i: (b, COL_V // KV_W)),
            pl.BlockSpec((None, None, past, KV_W), lambda b, i: (b, l, 0, 0)),
            pl.BlockSpec((None, None, past, KV_W), lambda b, i: (b, l, 0, 0)),
            pl.BlockSpec((1, HEAD_DIM), lambda b, i: (0, 0)),
            pl.BlockSpec((1, HEAD_DIM), lambda b, i: (0, 0)),
            pl.BlockSpec((tq, HEAD_DIM), lambda b, i: (i, 0)),
            pl.BlockSpec((tq, HEAD_DIM), lambda b, i: (i, 0)),
            pl.BlockSpec((t, HEAD_DIM), lambda b, i: (0, 0)),
            pl.BlockSpec((t, HEAD_DIM), lambda b, i: (0, 0)),
        ],
        out_specs=pl.BlockSpec((tq, ATTN_W), lambda b, i: (b * nq + i, 0)),
        out_shape=jax.ShapeDtypeStruct((bsz * t, ATTN_W), BF16),
        compiler_params=_cparams(("parallel", "arbitrary")),
        name="attn_lat",
    )(proj, proj, proj, cache_k, cache_v, q_norm_w, k_norm_w, cos_t, sin_t, cos_t, sin_t)


def _rope_lane_tables(n_tok):
    rows = n_tok // GRID_W
    row = jnp.repeat(jnp.arange(rows), GRID_W).astype(F32)
    colp = jnp.tile(jnp.arange(GRID_W), rows).astype(F32)
    half = HEAD_DIM // 2
    inv_freq = ROPE_BASE ** (-jnp.arange(0, half, 2, dtype=F32) / half)
    ang_r = row[:, None] * inv_freq
    ang_c = colp[:, None] * inv_freq
    cos_t = jnp.concatenate([jnp.cos(ang_r), jnp.cos(ang_r), jnp.cos(ang_c), jnp.cos(ang_c)], axis=-1)
    sin_t = jnp.concatenate([-jnp.sin(ang_r), jnp.sin(ang_r), -jnp.sin(ang_c), jnp.sin(ang_c)], axis=-1)
    return cos_t, sin_t


def _outproj_kernel(og_ref, oa_ref, wg_ref, wa_ref, x_ref, m_ref, g_ref, b_ref, o_ref):
    mix = _dot(og_ref[...], wg_ref[...]) + _dot(oa_ref[...], wa_ref[...])
    y = DN_ALPHA * x_ref[...] + m_ref[2:3, :] * mix
    o_ref[...] = _layer_norm(y, g_ref[...], b_ref[...])


def _out_proj(o_gdn, o_att, w_o, l, x, mod, ln_g, ln_b, tiles_per_group):
    n = x.shape[0]
    row = lambda i: (i, 0)
    const = lambda i: (0, 0)
    return pl.pallas_call(
        _outproj_kernel,
        grid=(n // TM,),
        in_specs=[
            pl.BlockSpec((TM, GDN_W), row),
            pl.BlockSpec((TM, ATTN_W), row),
            pl.BlockSpec((None, GDN_W, D_MODEL), lambda i: (l, 0, 0)),
            pl.BlockSpec((None, ATTN_W, D_MODEL), lambda i: (l, 1, 0)),
            pl.BlockSpec((TM, D_MODEL), row),
            pl.BlockSpec((None, 6, D_MODEL), lambda i: (i // tiles_per_group, 0, 0)),
            pl.BlockSpec((1, D_MODEL), const),
            pl.BlockSpec((1, D_MODEL), const),
        ],
        out_specs=pl.BlockSpec((TM, D_MODEL), row),
        out_shape=jax.ShapeDtypeStruct((n, D_MODEL), F32),
        compiler_params=_cparams(("parallel",)),
        name="out_proj",
    )(o_gdn, o_att, w_o, w_o, x, mod, ln_g, ln_b)


def _ffn_up_kernel(x_ref, m_ref, wg_ref, wu_ref, o_ref, h_ref):
    @pl.when(pl.program_id(1) == 0)
    def _():
        h_ref[...] = (x_ref[...] * (1.0 + m_ref[4:5, :]) + m_ref[3:4, :]).astype(BF16)

    h = h_ref[...]
    gate = _dot(h, wg_ref[...].astype(BF16))
    up = _dot(h, wu_ref[...].astype(BF16))
    o_ref[...] = (_silu(gate) * up).astype(o_ref.dtype)


def _ffn_up(x, mod, w_gu, l, rows_per_group):
    n = x.shape[0]
    nj = D_FF // TN_FF
    tm = min(TM_UP, rows_per_group)
    tiles_per_group = rows_per_group // tm
    return pl.pallas_call(
        _ffn_up_kernel,
        grid=(n // tm, nj),
        in_specs=[
            pl.BlockSpec((tm, D_MODEL), lambda i, j: (i, 0)),
            pl.BlockSpec((None, 6, D_MODEL), lambda i, j: (i // tiles_per_group, 0, 0)),
            pl.BlockSpec((None, D_MODEL, TN_FF), lambda i, j: (l, 0, j)),
            pl.BlockSpec((None, D_MODEL, TN_FF), lambda i, j: (l, 0, nj + j)),
        ],
        out_specs=pl.BlockSpec((tm, TN_FF), lambda i, j: (i, j)),
        out_shape=jax.ShapeDtypeStruct((n, D_FF), BF16),
        scratch_shapes=[pltpu.VMEM((tm, D_MODEL), BF16)],
        compiler_params=_cparams(("parallel", "arbitrary")),
        name="ffn_up",
    )(x, mod, w_gu, w_gu)


def _ffn_down_kernel(a_ref, w_ref, x_ref, m_ref, g_ref, b_ref, o_ref, acc_ref):
    k = pl.program_id(1)

    part = _dot(a_ref[...], w_ref[...])

    last = pl.num_programs(1) - 1

    @pl.when(k == 0)
    def _():
        acc_ref[...] = part

    @pl.when((k > 0) & (k < last))
    def _():
        acc_ref[...] += part

    @pl.when(k == last)
    def _():
        y = DN_ALPHA * x_ref[...] + m_ref[5:6, :] * (acc_ref[...] + part)
        o_ref[...] = _layer_norm(y, g_ref[...], b_ref[...])


def _ffn_down(act, w_down, l, x, mod, ln_g, ln_b, tiles_per_group):
    n = x.shape[0]
    return pl.pallas_call(
        _ffn_down_kernel,
        grid=(n // TM, D_FF // TK_FF),
        in_specs=[
            pl.BlockSpec((TM, TK_FF), lambda i, k: (i, k)),
            pl.BlockSpec((None, TK_FF, D_MODEL), lambda i, k: (l, k, 0)),
            pl.BlockSpec((TM, D_MODEL), lambda i, k: (i, 0)),
            pl.BlockSpec((None, 6, D_MODEL), lambda i, k: (i // tiles_per_group, 0, 0)),
            pl.BlockSpec((1, D_MODEL), lambda i, k: (0, 0)),
            pl.BlockSpec((1, D_MODEL), lambda i, k: (0, 0)),
        ],
        out_specs=pl.BlockSpec((TM, D_MODEL), lambda i, k: (i, 0)),
        out_shape=jax.ShapeDtypeStruct((n, D_MODEL), F32),
        scratch_shapes=[pltpu.VMEM((TM, D_MODEL), F32)],
        compiler_params=_cparams(("parallel", "arbitrary")),
        name="ffn_down",
    )(act, w_down, x, mod, ln_g, ln_b)


def _trunk_layer(x, mod, lw, l, bsz, t, latent, ctx_out):
    assert t % TM == 0 or TM % t == 0
    tiles_per_group = max(t // TM, 1) if latent is not None else x.shape[0] // TM
    rows_per_group = tiles_per_group * TM
    proj = _in_proj(x, mod, lw["w_in"], l, rows_per_group, HEAD_W, TN_HEAD)
    tail = _in_proj(x, mod, lw["w_tail"], l, rows_per_group, TAIL_W, TAIL_W)
    if latent is None:
        kc_all, vc_all, s_all = ctx_out
        o_gdn, s_all = _gdn(proj, tail, lw["conv_w"], lw["a_row"], lw["d_row"], lw["gdn_norm_w"], None, s_all, l, bsz, t)
        o_att, kc_all, vc_all = _attn_ctx(tail, lw["q_norm_w"], lw["k_norm_w"], kc_all, vc_all, l, bsz, t)
        ctx_out = (kc_all, vc_all, s_all)
    else:
        cache_k, cache_v, state_gdn, cos_t, sin_t = latent
        (o_gdn,) = _gdn(proj, tail, lw["conv_w"], lw["a_row"], lw["d_row"], lw["gdn_norm_w"], state_gdn, None, l, bsz, t)
        o_att = _attn_lat(tail, cache_k, cache_v, lw["q_norm_w"], lw["k_norm_w"], cos_t, sin_t, l, bsz, t, 256)
    x1 = _out_proj(o_gdn, o_att, lw["w_o"], l, x, mod, lw["ln1_g"], lw["ln1_b"], tiles_per_group)
    act = _ffn_up(x1, mod, lw["w_gate_up"], l, rows_per_group)
    x2 = _ffn_down(act, lw["w_down"], l, x1, mod, lw["ln2_g"], lw["ln2_b"], tiles_per_group)
    return x2, ctx_out


def kernel(x_prompt, x_sample, cache_k, cache_v, state_gdn, c, c_ctx, w_ada, b_ada, w_in, conv_w, a_log, dt_bias,
           gdn_norm_w, q_norm_w, k_norm_w, w_o, ln1_g, ln1_b, ln2_g, ln2_b, w_gate_up, w_down):
    bsz, seq, _ = x_prompt.shape
    dbsz, dseq, _ = x_sample.shape
    past = cache_k.shape[2]

    i_q = HEAD_W + 4 * H_GDN
    w_tail = jnp.concatenate(
        [w_in[:, :, i_q:], w_in[:, :, HEAD_W:i_q],
         jnp.zeros((DEPTH, D_MODEL, LANE - 4 * H_GDN), w_in.dtype)], axis=-1).astype(BF16)
    w_o_b = w_o.astype(BF16)
    w_down_b = w_down.astype(BF16)

    cond8 = jnp.concatenate([c_ctx[None, :], c, jnp.zeros((8 - 1 - dbsz, D_MODEL), F32)], axis=0)
    mod = _ada(cond8, w_ada, b_ada.reshape(DEPTH, 1, 6 * D_MODEL)).reshape(DEPTH, 8, 6, D_MODEL)

    cos_t, sin_t = _rope_lane_tables(dseq)
    ck = cache_k.reshape(dbsz, DEPTH, past, KV_W)
    cv = cache_v.reshape(dbsz, DEPTH, past, KV_W)

    def gate_rows(p):
        return jnp.zeros((1, LANE), F32).at[0, 2 * H_GDN:4 * H_GDN].set(p.reshape(-1))

    xp = x_prompt.reshape(bsz * seq, D_MODEL)
    xs = x_sample.reshape(dbsz * dseq, D_MODEL)
    ctx_out = (jnp.zeros((bsz, DEPTH, seq, KV_W), F32), jnp.zeros((bsz, DEPTH, seq, KV_W), F32),
               jnp.zeros((bsz, DEPTH, 2, H_GDN, DK, DK), F32))
    for l in range(DEPTH):
        lw = {
            "w_in": w_in, "w_tail": w_tail, "conv_w": conv_w[l], "a_row": gate_rows(a_log[l]), "d_row": gate_rows(dt_bias[l]),
            "gdn_norm_w": gdn_norm_w[l].reshape(1, DK), "q_norm_w": q_norm_w[l].reshape(1, HEAD_DIM),
            "k_norm_w": k_norm_w[l].reshape(1, HEAD_DIM), "w_o": w_o_b,
            "ln1_g": ln1_g[l].reshape(1, D_MODEL), "ln1_b": ln1_b[l].reshape(1, D_MODEL),
            "ln2_g": ln2_g[l].reshape(1, D_MODEL), "ln2_b": ln2_b[l].reshape(1, D_MODEL),
            "w_gate_up": w_gate_up, "w_down": w_down_b,
        }
        xp, ctx_out = _trunk_layer(xp, mod[l, 0:1], lw, l, bsz, seq, None, ctx_out)
        xs, _ = _trunk_layer(xs, mod[l, 1:1 + dbsz], lw, l, dbsz, dseq, (ck, cv, state_gdn, cos_t, sin_t), None)
    kc_all, vc_all, s_all = ctx_out
    return (xp.reshape(bsz, seq, D_MODEL), xs.reshape(dbsz, dseq, D_MODEL),
            kc_all.reshape(bsz, DEPTH, seq, N_KV, HEAD_DIM), vc_all.reshape(bsz, DEPTH, seq, N_KV, HEAD_DIM), s_all)
```

```python
import functools
import math

import jax
import jax.numpy as jnp
from jax import lax
from jax.experimental import pallas as pl
from jax.experimental.pallas import tpu as pltpu

D_MODEL = 2048
DEPTH = 2
GRID_W = 64
DK = 128
H_GDN = 8
GDN_W = H_GDN * DK
HEAD_DIM = 128
N_Q = 8
N_KV = 2
GRP = N_Q // N_KV
ATTN_W = N_Q * HEAD_DIM
KV_W = N_KV * HEAD_DIM
CONV_K = 5
D_FF = 5632
ROPE_BASE = 10000.0
RMS_EPS = 1e-6
LN_EPS = 1e-5
DN_ALPHA = (2 * DEPTH) ** 0.25
QKV_W = 3 * GDN_W
IN_W = QKV_W + GDN_W + 4 * H_GDN + ATTN_W + 2 * KV_W

LANE = 128
GDN_CHUNK = 128
GDN_HB = 2
GDN_HB_SHORT = 8
GDN_SHORT_T = 256
GDN_CG = 4
CONV_PAD = 8
GDN_STATIC_CHUNKS = 8
GDN_GROUP_HEADS = 4
GDN_SKEW = 12
COL_Z = QKV_W
COL_Q = COL_Z + GDN_W
COL_K = COL_Q + ATTN_W
COL_V = COL_K + KV_W
COL_BA = COL_V + KV_W
IN_W_PAD = COL_BA + LANE

TM = 512
TM_UP = 1024
TM_IN = 1024
TN_IN = 1920
TN_FF = 512
TK_FF = 2816
TN_ADA = 1024
VMEM_LIMIT = 56 * 1024 * 1024

F32 = jnp.float32
BF16 = jnp.bfloat16


def _cparams(sem):
    return pltpu.CompilerParams(dimension_semantics=sem, vmem_limit_bytes=VMEM_LIMIT)


def _dot(a, b):
    return jnp.dot(a, b, preferred_element_type=F32)


def _dot_nt(a, b):
    return lax.dot_general(a, b, (((1,), (1,)), ((), ())), preferred_element_type=F32)


def _dot_tn(a, b):
    return lax.dot_general(a, b, (((0,), (0,)), ((), ())), preferred_element_type=F32)


def _sigmoid(x):
    return 1.0 / (1.0 + jnp.exp(-x))


def _silu(x):
    return x * _sigmoid(x)


def _layer_norm(y, g, b):
    mu = jnp.mean(y, axis=-1, keepdims=True)
    yc = y - mu
    var = jnp.mean(yc * yc, axis=-1, keepdims=True)
    return yc * lax.rsqrt(var + LN_EPS) * g + b


def _rms_norm(x, w):
    return x * lax.rsqrt(jnp.mean(x * x, axis=-1, keepdims=True) + RMS_EPS) * w


def _ada_kernel(c_ref, w_ref, b_ref, o_ref):
    c = c_ref[...]
    o_ref[...] = _dot(_silu(c).astype(BF16), w_ref[...].astype(BF16)) + b_ref[...]


def _ada(cond8, w_ada, b_ada):
    n = w_ada.shape[-1]
    return pl.pallas_call(
        _ada_kernel,
        grid=(DEPTH, n // TN_ADA),
        in_specs=[
            pl.BlockSpec((8, D_MODEL), lambda l, j: (0, 0)),
            pl.BlockSpec((None, D_MODEL, TN_ADA), lambda l, j: (l, 0, j)),
            pl.BlockSpec((None, 1, TN_ADA), lambda l, j: (l, 0, j)),
        ],
        out_specs=pl.BlockSpec((None, 8, TN_ADA), lambda l, j: (l, 0, j)),
        out_shape=jax.ShapeDtypeStruct((DEPTH, 8, n), F32),
        compiler_params=_cparams(("arbitrary", "arbitrary")),
        name="ada",
    )(cond8, w_ada, b_ada)


def _inproj_kernel(x_ref, m_ref, w_ref, o_ref, h_ref):
    @pl.when(pl.program_id(1) == 0)
    def _():
        h_ref[...] = (x_ref[...] * (1.0 + m_ref[1:2, :]) + m_ref[0:1, :]).astype(BF16)

    o_ref[...] = _dot(h_ref[...], w_ref[...])


def _in_proj(x, mod, w, l, rows_per_group):
    n = x.shape[0]
    tm = min(TM_IN, rows_per_group)
    tiles_per_group = rows_per_group // tm
    return pl.pallas_call(
        _inproj_kernel,
        grid=(n // tm, IN_W_PAD // TN_IN),
        in_specs=[
            pl.BlockSpec((tm, D_MODEL), lambda i, j: (i, 0)),
            pl.BlockSpec((None, 6, D_MODEL), lambda i, j: (i // tiles_per_group, 0, 0)),
            pl.BlockSpec((None, D_MODEL, TN_IN), lambda i, j: (l, 0, j)),
        ],
        out_specs=pl.BlockSpec((tm, TN_IN), lambda i, j: (i, j)),
        out_shape=jax.ShapeDtypeStruct((n, IN_W_PAD), F32),
        scratch_shapes=[pltpu.VMEM((tm, D_MODEL), BF16)],
        compiler_params=_cparams(("parallel", "arbitrary")),
        name="in_proj",
    )(x, mod, w)


def _conv_silu(xp, i, t, w_ref, cols):
    acc = None
    for j in range(CONV_K):
        start = CONV_PAD - CONV_K // 2 + j
        term = xp[i, start:start + t, :] * w_ref[j:j + 1, cols]
        acc = term if acc is None else acc + term
    return _silu(acc)


def _seg_cumsum(x, reverse):
    t = x.shape[0]
    pos = lax.broadcasted_iota(jnp.int32, x.shape, 0) % GDN_CHUNK
    s = 1
    while s < GDN_CHUNK:
        if reverse:
            x = x + jnp.where(pos < GDN_CHUNK - s, pltpu.roll(x, shift=t - s, axis=0), 0.0)
        else:
            x = x + jnp.where(pos >= s, pltpu.roll(x, shift=s, axis=0), 0.0)
        s *= 2
    return x


def _unit_tri_inverses(lms, out):
    c = lms[0].shape[0]
    row = lax.broadcasted_iota(jnp.int32, (c, c), 0)
    col = lax.broadcasted_iota(jnp.int32, (c, c), 1)
    rc = row ^ col
    eye = (row == col).astype(F32)
    zero = jnp.zeros((c, c), BF16)
    xs = [eye - jnp.where(rc == 1, lm, 0.0) for lm in lms]
    lbs = [lm.astype(BF16) for lm in lms]
    for lvl in range(1, int(math.log2(c))):
        keep = (rc >> lvl) == 1
        xbs = [x.astype(BF16) for x in xs]
        ys = [_dot(xb, jnp.where(keep, lb, zero)).astype(BF16) for xb, lb in zip(xbs, lbs)]
        yield
        xs = [x - _dot(y, xb) for x, y, xb in zip(xs, ys, xbs)]
        yield
    out.extend(xs)


def _gdn_intra_stages(qs, ks, vs, gs, bs, us, wb, qb, ab, kd, chunks):
    c = GDN_CHUNK
    row = lax.broadcasted_iota(jnp.int32, (c, c), 0)
    col = lax.broadcasted_iota(jnp.int32, (c, c), 1)
    incl = (row >= col, row <= col)
    strict = (row > col, row < col)
    kqs = []
    for j, r0 in chunks:
        rows = pl.ds(r0, c)
        kcb = ks[j, rows, :].astype(BF16)
        kqs.append(_dot_nt(jnp.concatenate([kcb, qs[j, rows, :].astype(BF16)], axis=0), kcb))
    yield
    work, lms = [], []
    for (j, r0), kq in zip(chunks, kqs):
        rows = pl.ds(r0, c)
        for d in range(2):
            gcum = gs[j, d, rows, :]
            beta = bs[j, d, rows, :]
            dm = gcum - gcum.T
            decay = jnp.where(incl[d], jnp.exp(jnp.where(incl[d], dm, 0.0)), 0.0)
            lms.append(jnp.where(strict[d], kq[:c] * beta * decay, 0.0))
            ab[j, d, rows, :] = (kq[c:] * decay).astype(BF16)
            work.append((j, d, rows, gcum, beta))
    xs = []
    yield from _unit_tri_inverses(lms, xs)
    for (j, d, rows, gcum, beta), x in zip(work, xs):
        qc = qs[j, rows, :]
        kc = ks[j, rows, :]
        egc = jnp.exp(gcum)
        rhs = jnp.concatenate([vs[j, rows, :] * beta, kc * (beta * egc)], axis=1)
        sol = _dot(x.astype(BF16), rhs.astype(BF16))
        g_last = gcum[c - 1:c, :] if d == 0 else gcum[0:1, :]
        us[j, d, rows, :] = sol[:, :DK]
        wb[j, d, rows, :] = sol[:, DK:].astype(BF16)
        qb[j, d, rows, :] = (qc * egc).astype(BF16)
        kd[j, d, rows, :] = (kc * jnp.exp(g_last - gcum)).astype(BF16)
    yield


def _gdn_scan_stages(gs, us, wb, qb, ab, kd, ss, os, items, zero_state):
    c = GDN_CHUNK
    mids = []
    for j, d, r0 in items:
        rows = pl.ds(r0, c)
        if zero_state:
            mids.append((us[j, d, rows, :].astype(BF16), None, None))
        else:
            s_state = ss[j, d]
            ws_qs = _dot(jnp.concatenate([wb[j, d, rows, :], qb[j, d, rows, :]], axis=0), s_state.astype(BF16))
            mids.append(((us[j, d, rows, :] - ws_qs[:c]).astype(BF16), ws_qs[c:], s_state))
    yield
    for (j, d, r0), (v_new, q_s, s_state) in zip(items, mids):
        rows = pl.ds(r0, c)
        o = _dot(ab[j, d, rows, :], v_new)
        os[j, rows, :] += o if zero_state else q_s + o
        upd = _dot_tn(kd[j, d, rows, :], v_new)
        if zero_state:
            ss[j, d] = upd
        else:
            g_last = gs[j, d, pl.ds(r0 + (c - 1 if d == 0 else 0), 1), :]
            ss[j, d] = s_state * jnp.exp(g_last) + upd
    yield


def _run(gen):
    for _ in gen:
        pass


def _run_skewed(gens, skew):
    live, started, rnd = [], 0, 0
    while started < len(gens) or live:
        if started < len(gens) and rnd >= started * skew:
            live.append(gens[started])
            started += 1
        for g in list(live):
            try:
                next(g)
            except StopIteration:
                live.remove(g)
        rnd += 1


def _gdn_kernel(q_ref, k_ref, v_ref, z_ref, ba_ref, arow_ref, drow_ref, cwq_ref, cwk_ref, cwv_ref, nw_ref, *rest,
                has_s0, has_sout):
    rest = list(rest)
    s0_ref = rest.pop(0) if has_s0 else None
    if has_sout:
        rest.pop(0)
    o_ref = rest.pop(0)
    sout_ref = rest.pop(0) if has_sout else None
    xp, qs, ks, vs, gate, gs, bs, us, wb, qb, ab, kd, ss, os = rest
    hp = pl.program_id(1)
    t = q_ref.shape[0]
    hb = q_ref.shape[1] // DK
    lane = lax.broadcasted_iota(jnp.int32, (t, LANE), 1)

    @pl.when(hp == 0)
    def _():
        ba = ba_ref[...]
        x = ba + drow_ref[...]
        softplus = jnp.maximum(x, 0.0) + jnp.log(1.0 + jnp.exp(-jnp.abs(x)))
        g = -jnp.exp(arow_ref[...]) * softplus
        gcum = jnp.where(lane >= 3 * H_GDN, _seg_cumsum(g, True), _seg_cumsum(g, False))
        gate[...] = jnp.where(lane < 2 * H_GDN, _sigmoid(ba), gcum)

    n_chunks = t // GDN_CHUNK
    scan_refs = (gs, us, wb, qb, ab, kd, ss, os)
    intra_refs = (qs, ks, vs, gs, bs, us, wb, qb, ab, kd)

    def prologue(j):
        h = hp * hb + j
        sl = slice(j * DK, (j + 1) * DK)
        pad0 = jnp.zeros((CONV_PAD, DK), F32)
        for i, ref in enumerate((q_ref, k_ref, v_ref)):
            xp[i * hb + j, 0:CONV_PAD, :] = pad0
            xp[i * hb + j, CONV_PAD + t:, :] = pad0
            xp[i * hb + j, CONV_PAD:CONV_PAD + t, :] = ref[:, sl]
        q = _conv_silu(xp, j, t, cwq_ref, sl)
        qs[j] = q * lax.rsqrt(jnp.sum(q * q, axis=-1, keepdims=True) + RMS_EPS) * (DK ** -0.5)
        k = _conv_silu(xp, hb + j, t, cwk_ref, sl)
        ks[j] = k * lax.rsqrt(jnp.sum(k * k, axis=-1, keepdims=True) + RMS_EPS)
        vs[j] = _conv_silu(xp, 2 * hb + j, t, cwv_ref, sl)
        gt = gate[...]
        for d in range(2):
            bcol = jnp.sum(jnp.where(lane == d * H_GDN + h, gt, 0.0), axis=-1, keepdims=True)
            gcol = jnp.sum(jnp.where(lane == (2 + d) * H_GDN + h, gt, 0.0), axis=-1, keepdims=True)
            bs[j, d] = jnp.broadcast_to(bcol, (t, LANE))
            gs[j, d] = jnp.broadcast_to(gcol, (t, LANE))
            if has_s0:
                ss[j, d] = s0_ref[d, j]
        os[j] = jnp.zeros((t, DK), F32)

    def epilogue(j):
        sl = slice(j * DK, (j + 1) * DK)
        o_ref[:, sl] = (_rms_norm(os[j], nw_ref[...]) * _silu(z_ref[:, sl])).astype(o_ref.dtype)
        if has_sout:
            for d in range(2):
                sout_ref[d, j] = ss[j, d]

    def scan_items(i, js):
        items = []
        for j in js:
            for d in range(2):
                r0 = (i if d == 0 else n_chunks - 1 - i) * GDN_CHUNK
                items.append((j, d, r0 if isinstance(r0, int) else pl.multiple_of(r0, GDN_CHUNK)))
        return items

    if n_chunks <= GDN_STATIC_CHUNKS:
        def group_stages(js):
            for j in js:
                prologue(j)
                yield
            yield from _gdn_intra_stages(*intra_refs, [(j, cc * GDN_CHUNK) for j in js for cc in range(n_chunks)])
            for i in range(n_chunks):
                yield from _gdn_scan_stages(*scan_refs, scan_items(i, js), zero_state=(i == 0 and not has_s0))
            for j in js:
                epilogue(j)
                yield

        per = min(GDN_GROUP_HEADS, max(hb // 2, 1))
        n_groups = hb // per
        _run_skewed([group_stages(range(g * per, (g + 1) * per)) for g in range(n_groups)], GDN_SKEW)
        return

    for j in range(hb):
        prologue(j)
    cg = min(GDN_CG, n_chunks)

    def intra_group(i, carry):
        chunks = [(j, pl.multiple_of((i * cg + cc) * GDN_CHUNK, GDN_CHUNK)) for j in range(hb) for cc in range(cg)]
        _run(_gdn_intra_stages(*intra_refs, chunks))
        return carry

    def scan_step(i, carry):
        _run(_gdn_scan_stages(*scan_refs, scan_items(i, range(hb)), zero_state=False))
        return carry

    lax.fori_loop(0, n_chunks // cg, intra_group, 0)
    _run(_gdn_scan_stages(*scan_refs, scan_items(0, range(hb)), zero_state=not has_s0))
    lax.fori_loop(1, n_chunks, scan_step, 0)
    for j in range(hb):
        epilogue(j)


def _gdn(proj, conv_w, a_row, d_row, norm_w, s0, s_all, l, bsz, t):
    has_s0 = s0 is not None
    want_state = s_all is not None
    hb = GDN_HB_SHORT if t <= GDN_SHORT_T else GDN_HB
    wblk = hb * DK
    col = lambda off: (lambda b, hp: (b, off // wblk + hp))
    row0 = lambda b, hp: (0, 0)
    in_specs = [
        pl.BlockSpec((t, wblk), col(0)),
        pl.BlockSpec((t, wblk), col(GDN_W)),
        pl.BlockSpec((t, wblk), col(2 * GDN_W)),
        pl.BlockSpec((t, wblk), col(COL_Z)),
        pl.BlockSpec((t, LANE), lambda b, hp: (b, COL_BA // LANE)),
        pl.BlockSpec((1, LANE), row0),
        pl.BlockSpec((1, LANE), row0),
        pl.BlockSpec((CONV_K, wblk), lambda b, hp: (0, hp)),
        pl.BlockSpec((CONV_K, wblk), lambda b, hp: (0, GDN_W // wblk + hp)),
        pl.BlockSpec((CONV_K, wblk), lambda b, hp: (0, 2 * GDN_W // wblk + hp)),
        pl.BlockSpec((1, DK), row0),
    ]
    args = [proj, proj, proj, proj, proj, a_row, d_row, conv_w, conv_w, conv_w, norm_w]
    if has_s0:
        in_specs.append(pl.BlockSpec((None, None, 2, hb, DK, DK), lambda b, hp: (b, l, 0, hp, 0, 0)))
        args.append(s0)
    out_specs = [pl.BlockSpec((t, wblk), lambda b, hp: (b, hp))]
    out_shape = [jax.ShapeDtypeStruct((bsz * t, GDN_W), BF16)]
    aliases = {}
    if want_state:
        aliases = {len(args): 1}
        in_specs.append(pl.BlockSpec(memory_space=pl.ANY))
        args.append(s_all)
        out_specs.append(pl.BlockSpec((None, None, 2, hb, DK, DK), lambda b, hp: (b, l, 0, hp, 0, 0)))
        out_shape.append(jax.ShapeDtypeStruct(s_all.shape, F32))
    per_head = lambda dt: pltpu.VMEM((hb, t, DK), dt)
    per_dir = lambda dt: pltpu.VMEM((hb, 2, t, DK), dt)
    return pl.pallas_call(
        functools.partial(_gdn_kernel, has_s0=has_s0, has_sout=want_state),
        grid=(bsz, H_GDN // hb),
        in_specs=in_specs,
        out_specs=out_specs,
        out_shape=out_shape,
        input_output_aliases=aliases,
        scratch_shapes=[
            pltpu.VMEM((3 * hb, t + 2 * CONV_PAD, DK), F32),
            per_head(F32), per_head(F32), per_head(F32),
            pltpu.VMEM((t, LANE), F32),
            per_dir(F32), per_dir(F32),
            per_dir(F32), per_dir(BF16), per_dir(BF16), per_dir(BF16), per_dir(BF16),
            pltpu.VMEM((hb, 2, DK, DK), F32),
            per_head(F32),
        ],
        compiler_params=_cparams(("parallel", "arbitrary")),
        name="gdn_lat" if has_s0 else "gdn_ctx",
    )(*args)


def _rope(x, cos, sin):
    lane = lax.broadcasted_iota(jnp.int32, x.shape, 1)
    quarter = HEAD_DIM // 4
    partner = jnp.where((lane % (2 * quarter)) < quarter,
                        pltpu.roll(x, shift=HEAD_DIM - quarter, axis=1),
                        pltpu.roll(x, shift=quarter, axis=1))
    return x * cos + partner * sin


def _attend(qs, key_blocks, value_blocks):
    scale = HEAD_DIM ** -0.5
    scores = [[_dot_nt(q, k) * scale for k in key_blocks] for q in qs]
    probs, dens = [], []
    for sc in scores:
        m = None
        for s in sc:
            bm = jnp.max(s, axis=-1, keepdims=True)
            m = bm if m is None else jnp.maximum(m, bm)
        ps = [jnp.exp(s - m) for s in sc]
        den = None
        for p in ps:
            r = jnp.sum(p, axis=-1, keepdims=True)
            den = r if den is None else den + r
        probs.append([p.astype(BF16) for p in ps])
        dens.append(den)
    outs = []
    for ps, den in zip(probs, dens):
        acc = None
        for p, v in zip(ps, value_blocks):
            pv = _dot(p, v)
            acc = pv if acc is None else acc + pv
        outs.append(acc / den)
    return outs


def _attn_ctx_kernel(q_ref, k_ref, v_ref, qw_ref, kw_ref, kc_all_ref, vc_all_ref, o_ref, kc_ref, vc_ref):
    del kc_all_ref, vc_all_ref
    vc_ref[...] = v_ref[...]
    for g in range(N_KV):
        gl = slice(g * HEAD_DIM, (g + 1) * HEAD_DIM)
        kg = _rms_norm(k_ref[:, gl], kw_ref[...])
        kc_ref[:, gl] = kg
        heads = range(g * GRP, (g + 1) * GRP)
        qs = [_rms_norm(q_ref[:, h * HEAD_DIM:(h + 1) * HEAD_DIM], qw_ref[...]).astype(BF16) for h in heads]
        outs = _attend(qs, [kg.astype(BF16)], [v_ref[:, gl].astype(BF16)])
        for h, o in zip(heads, outs):
            o_ref[:, h * HEAD_DIM:(h + 1) * HEAD_DIM] = o.astype(o_ref.dtype)


def _attn_ctx(proj, q_norm_w, k_norm_w, kc_all, vc_all, l, bsz, t):
    cache_spec = pl.BlockSpec((None, None, t, KV_W), lambda b: (b, l, 0, 0))
    return pl.pallas_call(
        _attn_ctx_kernel,
        grid=(bsz,),
        in_specs=[
            pl.BlockSpec((t, ATTN_W), lambda b: (b, COL_Q // ATTN_W)),
            pl.BlockSpec((t, KV_W), lambda b: (b, COL_K // KV_W)),
            pl.BlockSpec((t, KV_W), lambda b: (b, COL_V // KV_W)),
            pl.BlockSpec((1, HEAD_DIM), lambda b: (0, 0)),
            pl.BlockSpec((1, HEAD_DIM), lambda b: (0, 0)),
            pl.BlockSpec(memory_space=pl.ANY),
            pl.BlockSpec(memory_space=pl.ANY),
        ],
        out_specs=[pl.BlockSpec((t, ATTN_W), lambda b: (b, 0)), cache_spec, cache_spec],
        out_shape=[
            jax.ShapeDtypeStruct((bsz * t, ATTN_W), BF16),
            jax.ShapeDtypeStruct(kc_all.shape, F32),
            jax.ShapeDtypeStruct(vc_all.shape, F32),
        ],
        input_output_aliases={5: 1, 6: 2},
        compiler_params=_cparams(("parallel",)),
        name="attn_ctx",
    )(proj, proj, proj, q_norm_w, k_norm_w, kc_all, vc_all)


def _attn_lat_kernel(q_ref, k_ref, v_ref, ck_ref, cv_ref, qw_ref, kw_ref, cq_ref, sq_ref, ck_cos_ref, ck_sin_ref,
                     o_ref):
    for g in range(N_KV):
        gl = slice(g * HEAD_DIM, (g + 1) * HEAD_DIM)
        kg = _rope(_rms_norm(k_ref[:, gl], kw_ref[...]), ck_cos_ref[...], ck_sin_ref[...]).astype(BF16)
        heads = range(g * GRP, (g + 1) * GRP)
        qs = [_rope(_rms_norm(q_ref[:, h * HEAD_DIM:(h + 1) * HEAD_DIM], qw_ref[...]), cq_ref[...], sq_ref[...])
              .astype(BF16) for h in heads]
        outs = _attend(qs, [kg, ck_ref[:, gl].astype(BF16)], [v_ref[:, gl].astype(BF16), cv_ref[:, gl].astype(BF16)])
        for h, o in zip(heads, outs):
            o_ref[:, h * HEAD_DIM:(h + 1) * HEAD_DIM] = o.astype(o_ref.dtype)


def _attn_lat(proj, cache_k, cache_v, q_norm_w, k_norm_w, cos_t, sin_t, l, bsz, t, tq):
    past = cache_k.shape[2]
    nq = t // tq
    return pl.pallas_call(
        _attn_lat_kernel,
        grid=(bsz, nq),
        in_specs=[
            pl.BlockSpec((tq, ATTN_W), lambda b, i: (b * nq + i, COL_Q // ATTN_W)),
            pl.BlockSpec((t, KV_W), lambda b, i: (b, COL_K // KV_W)),
            pl.BlockSpec((t, KV_W), lambda b, i: (b, COL_V // KV_W)),
            pl.BlockSpec((None, None, past, KV_W), lambda b, i: (b, l, 0, 0)),
            pl.BlockSpec((None, None, past, KV_W), lambda b, i: (b, l, 0, 0)),
            pl.BlockSpec((1, HEAD_DIM), lambda b, i: (0, 0)),
            pl.BlockSpec((1, HEAD_DIM), lambda b, i: (0, 0)),
            pl.BlockSpec((tq, HEAD_DIM), lambda b, i: (i, 0)),
            pl.BlockSpec((tq, HEAD_DIM), lambda b, i: (i, 0)),
            pl.BlockSpec((t, HEAD_DIM), lambda b, i: (0, 0)),
            pl.BlockSpec((t, HEAD_DIM), lambda b, i: (0, 0)),
        ],
        out_specs=pl.BlockSpec((tq, ATTN_W), lambda b, i: (b * nq + i, 0)),
        out_shape=jax.ShapeDtypeStruct((bsz * t, ATTN_W), BF16),
        compiler_params=_cparams(("parallel", "arbitrary")),
        name="attn_lat",
    )(proj, proj, proj, cache_k, cache_v, q_norm_w, k_norm_w, cos_t, sin_t, cos_t, sin_t)


def _rope_lane_tables(n_tok):
    rows = n_tok // GRID_W
    row = jnp.repeat(jnp.arange(rows), GRID_W).astype(F32)
    colp = jnp.tile(jnp.arange(GRID_W), rows).astype(F32)
    half = HEAD_DIM // 2
    inv_freq = ROPE_BASE ** (-jnp.arange(0, half, 2, dtype=F32) / half)
    ang_r = row[:, None] * inv_freq
    ang_c = colp[:, None] * inv_freq
    cos_t = jnp.concatenate([jnp.cos(ang_r), jnp.cos(ang_r), jnp.cos(ang_c), jnp.cos(ang_c)], axis=-1)
    sin_t = jnp.concatenate([-jnp.sin(ang_r), jnp.sin(ang_r), -jnp.sin(ang_c), jnp.sin(ang_c)], axis=-1)
    return cos_t, sin_t


def _outproj_kernel(og_ref, oa_ref, wg_ref, wa_ref, x_ref, m_ref, g_ref, b_ref, o_ref):
    mix = _dot(og_ref[...], wg_ref[...]) + _dot(oa_ref[...], wa_ref[...])
    y = DN_ALPHA * x_ref[...] + m_ref[2:3, :] * mix
    o_ref[...] = _layer_norm(y, g_ref[...], b_ref[...])


def _out_proj(o_gdn, o_att, w_o, l, x, mod, ln_g, ln_b, tiles_per_group):
    n = x.shape[0]
    row = lambda i: (i, 0)
    const = lambda i: (0, 0)
    return pl.pallas_call(
        _outproj_kernel,
        grid=(n // TM,),
        in_specs=[
            pl.BlockSpec((TM, GDN_W), row),
            pl.BlockSpec((TM, ATTN_W), row),
            pl.BlockSpec((None, GDN_W, D_MODEL), lambda i: (l, 0, 0)),
            pl.BlockSpec((None, ATTN_W, D_MODEL), lambda i: (l, 1, 0)),
            pl.BlockSpec((TM, D_MODEL), row),
            pl.BlockSpec((None, 6, D_MODEL), lambda i: (i // tiles_per_group, 0, 0)),
            pl.BlockSpec((1, D_MODEL), const),
            pl.BlockSpec((1, D_MODEL), const),
        ],
        out_specs=pl.BlockSpec((TM, D_MODEL), row),
        out_shape=jax.ShapeDtypeStruct((n, D_MODEL), F32),
        compiler_params=_cparams(("parallel",)),
        name="out_proj",
    )(o_gdn, o_att, w_o, w_o, x, mod, ln_g, ln_b)


def _ffn_up_kernel(x_ref, m_ref, wg_ref, wu_ref, o_ref, h_ref):
    @pl.when(pl.program_id(1) == 0)
    def _():
        h_ref[...] = (x_ref[...] * (1.0 + m_ref[4:5, :]) + m_ref[3:4, :]).astype(BF16)

    h = h_ref[...]
    gate = _dot(h, wg_ref[...].astype(BF16))
    up = _dot(h, wu_ref[...].astype(BF16))
    o_ref[...] = (_silu(gate) * up).astype(o_ref.dtype)


def _ffn_up(x, mod, w_gu, l, rows_per_group):
    n = x.shape[0]
    nj = D_FF // TN_FF
    tm = min(TM_UP, rows_per_group)
    tiles_per_group = rows_per_group // tm
    return pl.pallas_call(
        _ffn_up_kernel,
        grid=(n // tm, nj),
        in_specs=[
            pl.BlockSpec((tm, D_MODEL), lambda i, j: (i, 0)),
            pl.BlockSpec((None, 6, D_MODEL), lambda i, j: (i // tiles_per_group, 0, 0)),
            pl.BlockSpec((None, D_MODEL, TN_FF), lambda i, j: (l, 0, j)),
            pl.BlockSpec((None, D_MODEL, TN_FF), lambda i, j: (l, 0, nj + j)),
        ],
        out_specs=pl.BlockSpec((tm, TN_FF), lambda i, j: (i, j)),
        out_shape=jax.ShapeDtypeStruct((n, D_FF), BF16),
        scratch_shapes=[pltpu.VMEM((tm, D_MODEL), BF16)],
        compiler_params=_cparams(("parallel", "arbitrary")),
        name="ffn_up",
    )(x, mod, w_gu, w_gu)


def _ffn_down_kernel(a_ref, w_ref, x_ref, m_ref, g_ref, b_ref, o_ref, acc_ref):
    k = pl.program_id(1)

    part = _dot(a_ref[...], w_ref[...])

    last = pl.num_programs(1) - 1

    @pl.when(k == 0)
    def _():
        acc_ref[...] = part

    @pl.when((k > 0) & (k < last))
    def _():
        acc_ref[...] += part

    @pl.when(k == last)
    def _():
        y = DN_ALPHA * x_ref[...] + m_ref[5:6, :] * (acc_ref[...] + part)
        o_ref[...] = _layer_norm(y, g_ref[...], b_ref[...])


def _ffn_down(act, w_down, l, x, mod, ln_g, ln_b, tiles_per_group):
    n = x.shape[0]
    return pl.pallas_call(
        _ffn_down_kernel,
        grid=(n // TM, D_FF // TK_FF),
        in_specs=[
            pl.BlockSpec((TM, TK_FF), lambda i, k: (i, k)),
            pl.BlockSpec((None, TK_FF, D_MODEL), lambda i, k: (l, k, 0)),
            pl.BlockSpec((TM, D_MODEL), lambda i, k: (i, 0)),
            pl.BlockSpec((None, 6, D_MODEL), lambda i, k: (i // tiles_per_group, 0, 0)),
            pl.BlockSpec((1, D_MODEL), lambda i, k: (0, 0)),
            pl.BlockSpec((1, D_MODEL), lambda i, k: (0, 0)),
        ],
        out_specs=pl.BlockSpec((TM, D_MODEL), lambda i, k: (i, 0)),
        out_shape=jax.ShapeDtypeStruct((n, D_MODEL), F32),
        scratch_shapes=[pltpu.VMEM((TM, D_MODEL), F32)],
        compiler_params=_cparams(("parallel", "arbitrary")),
        name="ffn_down",
    )(act, w_down, x, mod, ln_g, ln_b)


def _trunk_layer(x, mod, lw, l, bsz, t, latent, ctx_out):
    assert t % TM == 0 or TM % t == 0
    tiles_per_group = max(t // TM, 1) if latent is not None else x.shape[0] // TM
    proj = _in_proj(x, mod, lw["w_in"], l, tiles_per_group * TM)
    if latent is None:
        kc_all, vc_all, s_all = ctx_out
        o_gdn, s_all = _gdn(proj, lw["conv_w"], lw["a_row"], lw["d_row"], lw["gdn_norm_w"], None, s_all, l, bsz, t)
        o_att, kc_all, vc_all = _attn_ctx(proj, lw["q_norm_w"], lw["k_norm_w"], kc_all, vc_all, l, bsz, t)
        ctx_out = (kc_all, vc_all, s_all)
    else:
        cache_k, cache_v, state_gdn, cos_t, sin_t = latent
        (o_gdn,) = _gdn(proj, lw["conv_w"], lw["a_row"], lw["d_row"], lw["gdn_norm_w"], state_gdn, None, l, bsz, t)
        o_att = _attn_lat(proj, cache_k, cache_v, lw["q_norm_w"], lw["k_norm_w"], cos_t, sin_t, l, bsz, t, 256)
    x1 = _out_proj(o_gdn, o_att, lw["w_o"], l, x, mod, lw["ln1_g"], lw["ln1_b"], tiles_per_group)
    act = _ffn_up(x1, mod, lw["w_gate_up"], l, tiles_per_group * TM)
    x2 = _ffn_down(act, lw["w_down"], l, x1, mod, lw["ln2_g"], lw["ln2_b"], tiles_per_group)
    return x2, ctx_out


def kernel(x_prompt, x_sample, cache_k, cache_v, state_gdn, c, c_ctx, w_ada, b_ada, w_in, conv_w, a_log, dt_bias,
           gdn_norm_w, q_norm_w, k_norm_w, w_o, ln1_g, ln1_b, ln2_g, ln2_b, w_gate_up, w_down):
    bsz, seq, _ = x_prompt.shape
    dbsz, dseq, _ = x_sample.shape
    past = cache_k.shape[2]

    i_b = QKV_W + GDN_W
    i_q = i_b + 4 * H_GDN
    w_in_r = jnp.concatenate(
        [w_in[:, :, :i_b], w_in[:, :, i_q:], w_in[:, :, i_b:i_q],
         jnp.zeros((DEPTH, D_MODEL, LANE - 4 * H_GDN), w_in.dtype)], axis=-1).astype(BF16)
    w_o_b = w_o.astype(BF16)
    w_down_b = w_down.astype(BF16)

    cond8 = jnp.concatenate([c_ctx[None, :], c, jnp.zeros((8 - 1 - dbsz, D_MODEL), F32)], axis=0)
    mod = _ada(cond8, w_ada, b_ada.reshape(DEPTH, 1, 6 * D_MODEL)).reshape(DEPTH, 8, 6, D_MODEL)

    cos_t, sin_t = _rope_lane_tables(dseq)
    ck = cache_k.reshape(dbsz, DEPTH, past, KV_W)
    cv = cache_v.reshape(dbsz, DEPTH, past, KV_W)

    def gate_rows(p):
        return jnp.zeros((1, LANE), F32).at[0, 2 * H_GDN:4 * H_GDN].set(p.reshape(-1))

    xp = x_prompt.reshape(bsz * seq, D_MODEL)
    xs = x_sample.reshape(dbsz * dseq, D_MODEL)
    ctx_out = (jnp.zeros((bsz, DEPTH, seq, KV_W), F32), jnp.zeros((bsz, DEPTH, seq, KV_W), F32),
               jnp.zeros((bsz, DEPTH, 2, H_GDN, DK, DK), F32))
    for l in range(DEPTH):
        lw = {
            "w_in": w_in_r, "conv_w": conv_w[l], "a_row": gate_rows(a_log[l]), "d_row": gate_rows(dt_bias[l]),
            "gdn_norm_w": gdn_norm_w[l].reshape(1, DK), "q_norm_w": q_norm_w[l].reshape(1, HEAD_DIM),
            "k_norm_w": k_norm_w[l].reshape(1, HEAD_DIM), "w_o": w_o_b,
            "ln1_g": ln1_g[l].reshape(1, D_MODEL), "ln1_b": ln1_b[l].reshape(1, D_MODEL),
            "ln2_g": ln2_g[l].reshape(1, D_MODEL), "ln2_b": ln2_b[l].reshape(1, D_MODEL),
            "w_gate_up": w_gate_up, "w_down": w_down_b,
        }
        xp, ctx_out = _trunk_layer(xp, mod[l, 0:1], lw, l, bsz, seq, None, ctx_out)
        xs, _ = _trunk_layer(xs, mod[l, 1:1 + dbsz], lw, l, dbsz, dseq, (ck, cv, state_gdn, cos_t, sin_t), None)
    kc_all, vc_all, s_all = ctx_out
    return (xp.reshape(bsz, seq, D_MODEL), xs.reshape(dbsz, dseq, D_MODEL),
            kc_all.reshape(bsz, DEPTH, seq, N_KV, HEAD_DIM), vc_all.reshape(bsz, DEPTH, seq, N_KV, HEAD_DIM), s_all)
```
